```python
import math
import jax, jax.numpy as jnp
from jax import lax
import numpy as np

D_MODEL = 1024
BATCH = 4
SEQ = 8192
DEPTH = 2
DEC_BATCH = 16
DEC_SEQ = 4096
PAST_LEN = 128

GRID_W = 64
WIN_H = 8
WIN_W = 16
ATT_HEADS = 8
HEAD_DIM = 64
ATT_WIDTH = ATT_HEADS * HEAD_DIM
SSM_GROUP = 16
SSM_GROUPS = 32
SSM_WIDTH = SSM_GROUP * SSM_GROUPS
SSM_STATE = 64
IN_WIDTH = 3 * ATT_WIDTH + SSM_WIDTH + 2 * D_MODEL
N_EXPERTS = 32
TOP_K = 4
D_EXPERT = 1024
SWIGLU_LIMIT = 7.0
SWIGLU_ALPHA = 1.702
EXPERT_BLOCK = 128
ALPHA = (2 * DEPTH) ** 0.25
BETA = (8 * DEPTH) ** -0.25
LN_EPS = 1e-5

kernel_name = "hybrid_natten_s5_moe_encoder"


def layer_norm(x, g, b):
    xf = x.astype(jnp.float32)
    mu = xf.mean(-1, keepdims=True)
    var = jnp.square(xf - mu).mean(-1, keepdims=True)
    y = (xf - mu) * lax.rsqrt(var + LN_EPS) * g.astype(jnp.float32) + b.astype(jnp.float32)
    return y.astype(x.dtype)


def neighbourhood_attention(q, k, v, rpb):
    B_, L_, _ = q.shape
    rows = L_ // GRID_W
    kh = min(WIN_H, rows)
    qg = q.reshape(B_, rows, GRID_W, ATT_HEADS, HEAD_DIM)
    kg = k.reshape(B_, rows, GRID_W, ATT_HEADS, HEAD_DIM)
    vg = v.reshape(B_, rows, GRID_W, ATT_HEADS, HEAD_DIM)
    cols = np.arange(GRID_W)
    col_start = np.clip(cols - WIN_W // 2, 0, GRID_W - WIN_W)
    col_idx = col_start[:, None] + np.arange(WIN_W)[None, :]
    col_off = col_idx - cols[:, None] + (WIN_W - 1)
    scale = HEAD_DIM ** -0.5

    def row_block(r):
        rs = jnp.clip(r - kh // 2, 0, rows - kh)
        q_r = lax.dynamic_index_in_dim(qg, r, axis=1, keepdims=False)
        k_band = lax.dynamic_slice_in_dim(kg, rs, kh, axis=1)
        v_band = lax.dynamic_slice_in_dim(vg, rs, kh, axis=1)
        k_win = k_band[:, :, col_idx]
        v_win = v_band[:, :, col_idx]
        row_off = rs + jnp.arange(kh) - r + (WIN_H - 1)
        bias = rpb[:, row_off][:, :, col_off]
        bias = jnp.transpose(bias, (0, 2, 1, 3)).astype(jnp.float32)
        s = jnp.einsum('bqhd,bkqjhd->bhqkj', q_r, k_win).astype(jnp.float32) * scale + bias[None]
        p = jax.nn.softmax(s.reshape(B_, ATT_HEADS, GRID_W, kh * WIN_W), axis=-1)
        p = p.reshape(s.shape).astype(v.dtype)
        return jnp.einsum('bhqkj,bkqjhd->bqhd', p, v_win)

    out = lax.map(row_block, jnp.arange(rows))
    return jnp.transpose(out, (1, 0, 2, 3, 4)).reshape(B_, L_, ATT_WIDTH)


def _diag_combine(e1, e2):
    a1r, a1i, b1r, b1i = e1
    a2r, a2i, b2r, b2i = e2
    ar = a2r * a1r - a2i * a1i
    ai = a2r * a1i + a2i * a1r
    br = a2r * b1r - a2i * b1i + b2r
    bi = a2r * b1i + a2i * b1r + b2i
    return (ar, ai, br, bi)


def s5_branch(u, lam_re, lam_im, log_step, b_re, b_im, c_re, c_im, d_skip, w_glu, b_glu):
    f32 = jnp.float32
    B_, L_, _ = u.shape
    uf = u.astype(f32).reshape(B_, L_, SSM_GROUPS, SSM_GROUP)
    ut = jnp.transpose(uf, (1, 0, 2, 3))
    y = d_skip.astype(f32) * uf
    for d in range(2):
        lr = lam_re[d].astype(f32)
        li = lam_im[d].astype(f32)
        step = jnp.exp(log_step[d].astype(f32))[:, None]
        mag = jnp.exp(lr * step)
        lb_re = mag * jnp.cos(li * step)
        lb_im = mag * jnp.sin(li * step)
        nr = lb_re - 1.0
        ni = lb_im
        den = lr * lr + li * li
        co_re = (nr * lr + ni * li) / den
        co_im = (ni * lr - nr * li) / den
        br_ = b_re[d].astype(f32)
        bi_ = b_im[d].astype(f32)
        bb_re = co_re[..., None] * br_ - co_im[..., None] * bi_
        bb_im = co_re[..., None] * bi_ + co_im[..., None] * br_
        bu_re = jnp.einsum('lbgh,gph->lbgp', ut, bb_re)
        bu_im = jnp.einsum('lbgh,gph->lbgp', ut, bb_im)
        a_re = jnp.broadcast_to(lb_re[None, None], (L_, 1, SSM_GROUPS, SSM_STATE))
        a_im = jnp.broadcast_to(lb_im[None, None], (L_, 1, SSM_GROUPS, SSM_STATE))
        _, _, xs_re, xs_im = lax.associative_scan(
            _diag_combine, (a_re, a_im, bu_re, bu_im), reverse=(d == 1), axis=0)
        y = y + jnp.einsum('lbgp,ghp->blgh', xs_re, c_re[d].astype(f32)) \
              - jnp.einsum('lbgp,ghp->blgh', xs_im, c_im[d].astype(f32))
    y = jax.nn.gelu(y.reshape(B_, L_, SSM_WIDTH)).astype(u.dtype)
    return y * jax.nn.sigmoid(y @ w_glu + b_glu)


def moe_ffn(h, w_router, b_router, w_up, b_up, w_down, b_down):
    B_, L_, D = h.shape
    T = B_ * L_
    TK = T * TOP_K
    xf = h.reshape(T, D)
    logits = (xf @ w_router + b_router).astype(jnp.float32)
    top_val, top_idx = lax.top_k(logits, TOP_K)
    gates = jax.nn.softmax(top_val, axis=-1)
    e_flat = top_idx.reshape(TK)
    tok_flat = jnp.repeat(jnp.arange(T, dtype=jnp.int32), TOP_K)
    g_flat = gates.reshape(TK)
    order = jnp.argsort(e_flat)
    e_sorted = e_flat[order]
    tok_sorted = tok_flat[order]
    g_sorted = g_flat[order]
    counts = jnp.bincount(e_flat, length=N_EXPERTS)
    padded = ((counts + EXPERT_BLOCK - 1) // EXPERT_BLOCK) * EXPERT_BLOCK
    start = jnp.cumsum(counts) - counts
    pend = jnp.cumsum(padded)
    pstart = pend - padded
    slot = pstart[e_sorted] + (jnp.arange(TK, dtype=jnp.int32) - start[e_sorted])
    n_blocks = -(-TK // EXPERT_BLOCK) + N_EXPERTS
    n_slots = n_blocks * EXPERT_BLOCK
    slot_tok = jnp.full((n_slots,), T, jnp.int32).at[slot].set(tok_sorted)
    slot_gate = jnp.zeros((n_slots,), h.dtype).at[slot].set(g_sorted.astype(h.dtype))
    block_expert = jnp.clip(
        jnp.searchsorted(pend, jnp.arange(n_blocks, dtype=jnp.int32) * EXPERT_BLOCK, side='right'),
        0, N_EXPERTS - 1)
    x_pad = jnp.concatenate([xf, jnp.zeros((1, D), xf.dtype)], axis=0)
    xs = x_pad[slot_tok].reshape(n_blocks, EXPERT_BLOCK, D)

    def expert_block(args):
        xb, e = args
        hu = xb @ w_up[e] + b_up[e]
        x_glu = jnp.minimum(hu[:, :D_EXPERT], SWIGLU_LIMIT)
        x_lin = jnp.clip(hu[:, D_EXPERT:], -SWIGLU_LIMIT, SWIGLU_LIMIT)
        act = x_glu * jax.nn.sigmoid(SWIGLU_ALPHA * x_glu) * (x_lin + 1.0)
        return act @ w_down[e] + b_down[e]

    out = lax.map(expert_block, (xs, block_expert)).reshape(n_slots, D)
    y = jnp.zeros((T + 1, D), h.dtype).at[slot_tok].add(out * slot_gate[:, None])
    return y[:T].reshape(B_, L_, D)


def trunk(x, w_in, rpb, lam_re, lam_im, log_step, b_re, b_im, c_re, c_im, d_skip, w_glu, b_glu,
          w_proj_a, w_proj_b, w_out, ln1_g, ln1_b, w_router, b_router, w_up, b_up, w_down, b_down,
          ln2_g, ln2_b):
    s_k = ATT_WIDTH
    s_v = 2 * ATT_WIDTH
    s_u = 3 * ATT_WIDTH
    s_ga = s_u + SSM_WIDTH
    s_gb = s_ga + D_MODEL
    for l in range(DEPTH):
        p = x @ w_in[l]
        q = p[..., :s_k]
        k = p[..., s_k:s_v]
        v = p[..., s_v:s_u]
        u = p[..., s_u:s_ga]
        ga = p[..., s_ga:s_gb]
        gb = p[..., s_gb:]
        y_a = neighbourhood_attention(q, k, v, rpb[l])
        y_b = s5_branch(u, lam_re[l], lam_im[l], log_step[l], b_re[l], b_im[l], c_re[l], c_im[l],
                        d_skip[l], w_glu[l], b_glu[l])
        mix = jax.nn.sigmoid(ga) * (y_a @ w_proj_a[l]) + jax.nn.sigmoid(gb) * (y_b @ w_proj_b[l])
        h = layer_norm(ALPHA * x + mix @ w_out[l], ln1_g[l], ln1_b[l])
        f = moe_ffn(h, w_router[l], b_router[l], w_up[l], b_up[l], w_down[l], b_down[l])
        x = layer_norm(ALPHA * h + f, ln2_g[l], ln2_b[l])
    return x


def setup_inputs(seed: int = 0) -> dict:
    key = jax.random.key(seed)
    ks = jax.random.split(key, 32)
    f32 = jnp.float32

    def nrm(k, shape, s):
        return jax.random.normal(k, shape, f32) * s

    x_prompt = nrm(ks[0], (BATCH, SEQ, D_MODEL), 1.0)
    x_sample = nrm(ks[1], (DEC_BATCH, DEC_SEQ, D_MODEL), 1.0)
    col_scale = jnp.concatenate([
        jnp.ones((2 * ATT_WIDTH,), f32),
        jnp.full((ATT_WIDTH,), BETA, f32),
        jnp.ones((SSM_WIDTH + 2 * D_MODEL,), f32)])
    w_in = nrm(ks[2], (DEPTH, D_MODEL, IN_WIDTH), D_MODEL ** -0.5) * col_scale
    rpb = nrm(ks[3], (DEPTH, ATT_HEADS, 2 * WIN_H - 1, 2 * WIN_W - 1), 0.02)
    n = jnp.arange(SSM_STATE, dtype=f32)
    ssm_shape = (DEPTH, 2, SSM_GROUPS, SSM_STATE)
    lam_re = -0.5 + nrm(ks[4], ssm_shape, 0.01)
    lam_im = math.pi * n + nrm(ks[5], ssm_shape, 0.01)
    log_step = jax.random.uniform(ks[6], (DEPTH, 2, SSM_GROUPS), f32, math.log(1e-3), math.log(1e-1))
    b_shape = (DEPTH, 2, SSM_GROUPS, SSM_STATE, SSM_GROUP)
    b_re = nrm(ks[7], b_shape, (2 * SSM_GROUP) ** -0.5)
    b_im = nrm(ks[8], b_shape, (2 * SSM_GROUP) ** -0.5)
    c_shape = (DEPTH, 2, SSM_GROUPS, SSM_GROUP, SSM_STATE)
    c_re = nrm(ks[9], c_shape, SSM_STATE ** -0.5)
    c_im = nrm(ks[10], c_shape, SSM_STATE ** -0.5)
    d_skip = nrm(ks[11], (DEPTH, SSM_GROUPS, SSM_GROUP), 1.0)
    w_glu = nrm(ks[12], (DEPTH, SSM_WIDTH, SSM_WIDTH), SSM_WIDTH ** -0.5)
    b_glu = nrm(ks[13], (DEPTH, SSM_WIDTH), 0.01)
    w_proj_a = nrm(ks[14], (DEPTH, ATT_WIDTH, D_MODEL), BETA * ATT_WIDTH ** -0.5)
    w_proj_b = nrm(ks[15], (DEPTH, SSM_WIDTH, D_MODEL), BETA * SSM_WIDTH ** -0.5)
    w_out = nrm(ks[16], (DEPTH, D_MODEL, D_MODEL), BETA * D_MODEL ** -0.5)
    ln1_g = 1.0 + nrm(ks[17], (DEPTH, D_MODEL), 0.01)
    ln1_b = nrm(ks[18], (DEPTH, D_MODEL), 0.01)
    w_router = nrm(ks[19], (DEPTH, D_MODEL, N_EXPERTS), D_MODEL ** -0.5)
    b_router = nrm(ks[20], (DEPTH, N_EXPERTS), 0.01)
    w_up = nrm(ks[21], (DEPTH, N_EXPERTS, D_MODEL, 2 * D_EXPERT), BETA * D_MODEL ** -0.5)
    b_up = nrm(ks[22], (DEPTH, N_EXPERTS, 2 * D_EXPERT), 0.01)
    w_down = nrm(ks[23], (DEPTH, N_EXPERTS, D_EXPERT, D_MODEL), BETA * D_EXPERT ** -0.5)
    b_down = nrm(ks[24], (DEPTH, N_EXPERTS, D_MODEL), 0.01)
    ln2_g = 1.0 + nrm(ks[25], (DEPTH, D_MODEL), 0.01)
    ln2_b = nrm(ks[26], (DEPTH, D_MODEL), 0.01)
    return {"x_prompt": x_prompt, "x_sample": x_sample, "w_in": w_in, "rpb": rpb,
            "lam_re": lam_re, "lam_im": lam_im, "log_step": log_step, "b_re": b_re, "b_im": b_im,
            "c_re": c_re, "c_im": c_im, "d_skip": d_skip, "w_glu": w_glu, "b_glu": b_glu,
            "w_proj_a": w_proj_a, "w_proj_b": w_proj_b, "w_out": w_out, "ln1_g": ln1_g, "ln1_b": ln1_b,
            "w_router": w_router, "b_router": b_router, "w_up": w_up, "b_up": b_up,
            "w_down": w_down, "b_down": b_down, "ln2_g": ln2_g, "ln2_b": ln2_b}


def reference(x_prompt, x_sample, w_in, rpb, lam_re, lam_im, log_step, b_re, b_im, c_re, c_im, d_skip,
              w_glu, b_glu, w_proj_a, w_proj_b, w_out, ln1_g, ln1_b, w_router, b_router, w_up, b_up,
              w_down, b_down, ln2_g, ln2_b):
    y_prompt = trunk(x_prompt, w_in, rpb, lam_re, lam_im, log_step, b_re, b_im, c_re, c_im, d_skip,
                     w_glu, b_glu, w_proj_a, w_proj_b, w_out, ln1_g, ln1_b, w_router, b_router,
                     w_up, b_up, w_down, b_down, ln2_g, ln2_b)
    y_sample = trunk(x_sample, w_in, rpb, lam_re, lam_im, log_step, b_re, b_im, c_re, c_im, d_skip,
                     w_glu, b_glu, w_proj_a, w_proj_b, w_out, ln1_g, ln1_b, w_router, b_router,
                     w_up, b_up, w_down, b_down, ln2_g, ln2_b)
    return (y_prompt, y_sample)
```

```python
import functools
import math

import numpy as np
import jax
import jax.numpy as jnp
from jax import lax
from jax.experimental import pallas as pl
from jax.experimental.pallas import tpu as pltpu

F32 = jnp.float32
BF16 = jnp.bfloat16
I32 = jnp.int32

D_MODEL = 1024
DEPTH = 2
GRID_W = 64
WIN_H = 8
WIN_W = 16
ATT_HEADS = 8
HEAD_DIM = 64
ATT_WIDTH = ATT_HEADS * HEAD_DIM
SSM_GROUP = 16
SSM_GROUPS = 32
SSM_WIDTH = SSM_GROUP * SSM_GROUPS
SSM_STATE = 64
IN_WIDTH = 3 * ATT_WIDTH + SSM_WIDTH + 2 * D_MODEL
N_EXPERTS = 32
TOP_K = 4
D_EXPERT = 1024
SWIGLU_LIMIT = 7.0
SWIGLU_ALPHA = 1.702
ALPHA = (2 * DEPTH) ** 0.25
LN_EPS = 1e-5

LANES = 128
SUBLANES = 8
ROW_TILES = D_MODEL // LANES
VMEM_LIMIT = 56 * 1024 * 1024

HEAD_PAIRS = ATT_HEADS // 2
Q_ROWS = 8
BAND_ROWS = Q_ROWS + WIN_H
SLAB_GROUPS = LANES // SSM_GROUP
N_SLABS = SSM_GROUPS // SLAB_GROUPS
SLAB_STATE = SLAB_GROUPS * SSM_STATE
SCAN_TILE = 256
EXPERT_ROWS = 512
ISSUE_UNROLL = 4
NEG_BIG = -1e30


def _cparams(sem):
    return pltpu.CompilerParams(dimension_semantics=sem, vmem_limit_bytes=VMEM_LIMIT)


def _layer_norm(z, g, b):
    mu = jnp.mean(z, axis=-1, keepdims=True)
    zc = z - mu
    var = jnp.mean(zc * zc, axis=-1, keepdims=True)
    return zc * lax.rsqrt(var + LN_EPS) * g + b


IN_TM = 512
IN_CHUNK = 1024


def _inproj_body(x_ref, w_ref, o_ref):
    xb = x_ref[...].astype(BF16)
    for c in range(IN_WIDTH // IN_CHUNK):
        sl = slice(c * IN_CHUNK, (c + 1) * IN_CHUNK)
        o_ref[:, sl] = jnp.dot(xb, w_ref[:, sl], preferred_element_type=F32).astype(BF16)


def _inproj(x2d, w):
    t = x2d.shape[0]
    return pl.pallas_call(
        _inproj_body,
        grid=(t // IN_TM,),
        in_specs=[pl.BlockSpec((IN_TM, D_MODEL), lambda i: (i, 0)),
                  pl.BlockSpec((D_MODEL, IN_WIDTH), lambda i: (0, 0))],
        out_specs=pl.BlockSpec((IN_TM, IN_WIDTH), lambda i: (i, 0)),
        out_shape=jax.ShapeDtypeStruct((t, IN_WIDTH), BF16),
        compiler_params=_cparams(("parallel",)),
        name="inproj",
    )(x2d, w)


def _attention_bias(rpb_l):
    cols = np.arange(GRID_W)
    col_start = np.clip(cols - WIN_W // 2, 0, GRID_W - WIN_W)
    kc = np.arange(GRID_W)
    inside = (kc[None, :] >= col_start[:, None]) & (kc[None, :] < col_start[:, None] + WIN_W)
    col_off = np.clip(kc[None, :] - cols[:, None] + (WIN_W - 1), 0, 2 * WIN_W - 2)
    delta = np.arange(WIN_H)
    j = np.arange(WIN_H)
    row_off = j[None, :] - delta[:, None] + (WIN_H - 1)
    b = rpb_l.astype(F32)[:, row_off]
    b = b[:, :, :, col_off]
    b = jnp.where(jnp.asarray(inside)[None, None, None], b, NEG_BIG)
    b = jnp.transpose(b, (0, 1, 3, 2, 4))
    b = b.reshape(HEAD_PAIRS, 2, WIN_H, GRID_W, WIN_H * GRID_W)
    b = jnp.transpose(b, (0, 2, 1, 3, 4))
    return b.reshape(HEAD_PAIRS, WIN_H, 2 * GRID_W, WIN_H * GRID_W)


def _natten_body(q_ref, k_ref, v_ref, bias_ref, o_ref, s_scr, p_scr, *, rows):
    r0 = pl.program_id(2) * Q_ROWS
    base = jnp.clip(r0 - WIN_H // 2, 0, rows - BAND_ROWS)
    lane = lax.broadcasted_iota(I32, (GRID_W, LANES), 1)
    first_head = lane < HEAD_DIM
    scale = HEAD_DIM ** -0.5
    nkeys = WIN_H * GRID_W
    nq = 2 * GRID_W

    def window(i):
        rs = jnp.clip(r0 + i - WIN_H // 2, 0, rows - WIN_H)
        return pl.multiple_of((rs - base) * GRID_W, GRID_W), r0 + i - rs

    for i in range(Q_ROWS):
        off, delta = window(i)
        qs = q_ref[i * GRID_W:(i + 1) * GRID_W, :] * scale
        zero = jnp.zeros_like(qs)
        qm = jnp.concatenate([jnp.where(first_head, qs, zero), jnp.where(first_head, zero, qs)], axis=0)
        s = lax.dot_general(qm, k_ref[pl.ds(off, nkeys), :], (((1,), (1,)), ((), ())),
                            preferred_element_type=F32)
        s_scr[i * nq:(i + 1) * nq, :] = s + bias_ref[delta]
    s = s_scr[...]
    e = jnp.exp(s - jnp.max(s, axis=1, keepdims=True))
    p_scr[...] = (e / jnp.sum(e, axis=1, keepdims=True)).astype(BF16)
    for i in range(Q_ROWS):
        off, _ = window(i)
        o = jnp.dot(p_scr[i * nq:(i + 1) * nq, :], v_ref[pl.ds(off, nkeys), :],
                    preferred_element_type=F32)
        oo = jnp.where(first_head, o[:GRID_W], o[GRID_W:])
        o_ref[i * GRID_W:(i + 1) * GRID_W, :] = oo.astype(BF16)


def _natten(p, bias, batch, seq):
    t = p.shape[0]
    rows = seq // GRID_W
    nrb = rows // Q_ROWS
    qtok = Q_ROWS * GRID_W
    btok = BAND_ROWS * GRID_W
    kcol = ATT_WIDTH // LANES
    vcol = 2 * ATT_WIDTH // LANES

    def band_start(b, rb):
        base = jnp.clip(rb * Q_ROWS - WIN_H // 2, 0, rows - BAND_ROWS)
        return (b * rows + base) * GRID_W

    return pl.pallas_call(
        functools.partial(_natten_body, rows=rows),
        grid=(HEAD_PAIRS, batch, nrb),
        in_specs=[
            pl.BlockSpec((qtok, LANES), lambda pr, b, rb: (b * nrb + rb, pr)),
            pl.BlockSpec((pl.Element(btok), pl.Element(LANES)),
                         lambda pr, b, rb: (band_start(b, rb), (kcol + pr) * LANES)),
            pl.BlockSpec((pl.Element(btok), pl.Element(LANES)),
                         lambda pr, b, rb: (band_start(b, rb), (vcol + pr) * LANES)),
            pl.BlockSpec((None, WIN_H, 2 * GRID_W, WIN_H * GRID_W), lambda pr, b, rb: (pr, 0, 0, 0)),
        ],
        out_specs=pl.BlockSpec((qtok, LANES), lambda pr, b, rb: (b * nrb + rb, pr)),
        out_shape=jax.ShapeDtypeStruct((t, ATT_WIDTH), BF16),
        scratch_shapes=[pltpu.VMEM((Q_ROWS * 2 * GRID_W, WIN_H * GRID_W), F32),
                        pltpu.VMEM((Q_ROWS * 2 * GRID_W, WIN_H * GRID_W), BF16)],
        compiler_params=_cparams(("parallel", "parallel", "parallel")),
        name="natten",
    )(p, p, p, bias)


def _s5_weights(lam_re, lam_im, log_step, b_re, b_im, c_re, c_im):
    lr = lam_re.astype(F32)
    li = lam_im.astype(F32)
    step = jnp.exp(log_step.astype(F32))[..., None]
    mag = jnp.exp(lr * step)
    lb_re = mag * jnp.cos(li * step)
    lb_im = mag * jnp.sin(li * step)
    nr = lb_re - 1.0
    ni = lb_im
    den = lr * lr + li * li
    co_re = (nr * lr + ni * li) / den
    co_im = (ni * lr - nr * li) / den
    br_ = b_re.astype(F32)
    bi_ = b_im.astype(F32)
    bb_re = co_re[..., None] * br_ - co_im[..., None] * bi_
    bb_im = co_re[..., None] * bi_ + co_im[..., None] * br_
    eye = jnp.eye(SLAB_GROUPS, dtype=F32)

    def block_in(bb):
        bb = bb.reshape(2, N_SLABS, SLAB_GROUPS, SSM_STATE, SSM_GROUP)
        m = jnp.einsum('dsgph,gk->dsghkp', bb, eye)
        return m.reshape(2, N_SLABS, SLAB_GROUPS * SSM_GROUP, SLAB_STATE)

    def block_out(cc):
        cc = cc.reshape(2, N_SLABS, SLAB_GROUPS, SSM_GROUP, SSM_STATE)
        m = jnp.einsum('dsghp,gk->dsgpkh', cc, eye)
        return m.reshape(2, N_SLABS, SLAB_STATE, SLAB_GROUPS * SSM_GROUP)

    in_w = jnp.concatenate([block_in(bb_re), block_in(bb_im)], axis=-1).astype(BF16)
    out_w = jnp.concatenate([block_out(c_re.astype(F32)), -block_out(c_im.astype(F32))], axis=-2).astype(BF16)
    a_re = lb_re.reshape(2, N_SLABS, 1, SLAB_STATE)
    a_im = lb_im.reshape(2, N_SLABS, 1, SLAB_STATE)
    return in_w, out_w, a_re, a_im


def _s5_body(uf_ref, ub_ref, inw_ref, outw_ref, are_ref, aim_ref, yf_ref, yb_ref,
             uperm, buf, yperm, car_f, car_b, *, nseq, nslab):
    tt = SCAN_TILE
    ns = SLAB_STATE

    @pl.when(pl.program_id(2) == 0)
    def _():
        car_f[...] = jnp.zeros_like(car_f)
        car_b[...] = jnp.zeros_like(car_b)
        if nslab > 1:
            uperm[...] = jnp.zeros_like(uperm)

    for d, (u_ref, y_ref, car) in enumerate(((uf_ref, yf_ref, car_f), (ub_ref, yb_ref, car_b))):
        for q in range(nslab):
            for s in range(nseq):
                uperm[q, pl.ds(q * nseq + s, tt, stride=SUBLANES), :] = (
                    u_ref[s, :, q * LANES:(q + 1) * LANES].astype(F32))
        lhs = jnp.concatenate([uperm[q] for q in range(nslab)], axis=1).astype(BF16)
        buf[...] = jnp.dot(lhs, inw_ref[d], preferred_element_type=F32)
        ar = are_ref[d]
        ai = aim_ref[d]

        def step(i, c, d=d, ar=ar, ai=ai):
            xr, xi = c
            t = i if d == 0 else tt - 1 - i
            row = pl.multiple_of(t * SUBLANES, SUBLANES)
            nr = ar * xr - ai * xi + buf[pl.ds(row, SUBLANES), 0:ns]
            ni = ar * xi + ai * xr + buf[pl.ds(row, SUBLANES), ns:2 * ns]
            buf[pl.ds(row, SUBLANES), 0:ns] = nr
            buf[pl.ds(row, SUBLANES), ns:2 * ns] = ni
            return nr, ni

        xr, xi = lax.fori_loop(0, tt, step, (car[:, 0:ns], car[:, ns:2 * ns]), unroll=8)
        car[:, 0:ns] = xr
        car[:, ns:2 * ns] = xi
        y2 = jnp.dot(buf[...].astype(BF16), outw_ref[d], preferred_element_type=F32)
        for q in range(nslab):
            yperm[q] = y2[:, q * LANES:(q + 1) * LANES]
        for q in range(nslab):
            for s in range(nseq):
                y_ref[s, :, q * LANES:(q + 1) * LANES] = yperm[q, pl.ds(q * nseq + s, tt, stride=SUBLANES), :]


def _s5_scan(p, s5w, batch, seq):
    in_w, out_w, a_re, a_im = s5w
    t = p.shape[0]
    nseq = min(batch, SUBLANES)
    ngroups = batch // nseq
    nslab = SUBLANES // nseq
    assert nslab * nseq == SUBLANES and N_SLABS % nslab == 0
    nstep = N_SLABS // nslab
    width = nslab * LANES
    nt = seq // SCAN_TILE
    ucol = 3 * ATT_WIDTH // width
    in_w = in_w.reshape(2, nstep, width, 2 * SLAB_STATE)
    out_w = jnp.transpose(out_w.reshape(2, nstep, nslab, 2 * SLAB_STATE, LANES), (0, 1, 3, 2, 4))
    out_w = out_w.reshape(2, nstep, 2 * SLAB_STATE, width)
    per_row = lambda a: jnp.repeat(a.reshape(2, nstep, nslab, SLAB_STATE), nseq, axis=2)
    a_re, a_im = per_row(a_re), per_row(a_im)
    p3 = p.reshape(batch, seq, IN_WIDTH)
    ublock = (nseq, SCAN_TILE, width)
    wspec = lambda shape: pl.BlockSpec((2, None) + shape, lambda g, s, j: (0, s, 0, 0))
    yf, yb = pl.pallas_call(
        functools.partial(_s5_body, nseq=nseq, nslab=nslab),
        grid=(ngroups, nstep, nt),
        in_specs=[
            pl.BlockSpec(ublock, lambda g, s, j: (g, j, ucol + s)),
            pl.BlockSpec(ublock, lambda g, s, j: (g, nt - 1 - j, ucol + s)),
            wspec((width, 2 * SLAB_STATE)),
            wspec((2 * SLAB_STATE, width)),
            wspec((SUBLANES, SLAB_STATE)),
            wspec((SUBLANES, SLAB_STATE)),
        ],
        out_specs=[pl.BlockSpec(ublock, lambda g, s, j: (g, j, s)),
                   pl.BlockSpec(ublock, lambda g, s, j: (g, nt - 1 - j, s))],
        out_shape=[jax.ShapeDtypeStruct((batch, seq, SSM_WIDTH), F32)] * 2,
        scratch_shapes=[pltpu.VMEM((nslab, SCAN_TILE * SUBLANES, LANES), F32),
                        pltpu.VMEM((SCAN_TILE * SUBLANES, 2 * SLAB_STATE), F32),
                        pltpu.VMEM((nslab, SCAN_TILE * SUBLANES, LANES), F32),
                        pltpu.VMEM((SUBLANES, 2 * SLAB_STATE), F32),
                        pltpu.VMEM((SUBLANES, 2 * SLAB_STATE), F32)],
        compiler_params=_cparams(("parallel", "parallel", "arbitrary")),
        name="s5_scan",
    )(p3, p3, in_w, out_w, a_re, a_im)
    return yf.reshape(t, SSM_WIDTH), yb.reshape(t, SSM_WIDTH)


MIX_TM = 256


def _mix_body(x_ref, u_ref, ga_ref, gb_ref, ya_ref, yf_ref, yb_ref, dsk_ref, wglu_ref, bglu_ref,
              wpa_ref, wpb_ref, wout_ref, g_ref, b_ref, h_ref, hrt_ref):
    y = dsk_ref[...] * u_ref[...].astype(F32) + yf_ref[...] + yb_ref[...]
    y = jax.nn.gelu(y)
    z = jnp.dot(y.astype(BF16), wglu_ref[...], preferred_element_type=F32) + bglu_ref[...]
    y = y * jax.nn.sigmoid(z)
    pa = jnp.dot(ya_ref[...], wpa_ref[...], preferred_element_type=F32)
    pb = jnp.dot(y.astype(BF16), wpb_ref[...], preferred_element_type=F32)
    mix = jax.nn.sigmoid(ga_ref[...].astype(F32)) * pa + jax.nn.sigmoid(gb_ref[...].astype(F32)) * pb
    m = jnp.dot(mix.astype(BF16), wout_ref[...], preferred_element_type=F32)
    h = _layer_norm(ALPHA * x_ref[...] + m, g_ref[...], b_ref[...])
    h_ref[...] = h
    _store_row_tiled(hrt_ref, h)


def _mix(x2d, p, ya, yf, yb, d_skip, w_glu, b_glu, w_pa, w_pb, w_out, ln_g, ln_b):
    t = x2d.shape[0]
    tm = MIX_TM
    row = lambda width, col=0: pl.BlockSpec((tm, width), lambda i: (i, col))
    full = lambda a: pl.BlockSpec(a.shape, lambda i: (0,) * a.ndim)
    consts = (d_skip, w_glu, b_glu, w_pa, w_pb, w_out, ln_g, ln_b)
    return pl.pallas_call(
        _mix_body,
        grid=(t // tm,),
        in_specs=[row(D_MODEL),
                  row(SSM_WIDTH, 3 * ATT_WIDTH // SSM_WIDTH),
                  row(D_MODEL, (3 * ATT_WIDTH + SSM_WIDTH) // D_MODEL),
                  row(D_MODEL, (3 * ATT_WIDTH + SSM_WIDTH) // D_MODEL + 1),
                  row(ATT_WIDTH), row(SSM_WIDTH), row(SSM_WIDTH)] + [full(a) for a in consts],
        out_specs=[row(D_MODEL), pl.BlockSpec((tm * ROW_TILES, LANES), lambda i: (i, 0))],
        out_shape=[jax.ShapeDtypeStruct((t, D_MODEL), F32),
                   jax.ShapeDtypeStruct((t * ROW_TILES, LANES), F32)],
        compiler_params=_cparams(("parallel",)),
        name="mix",
    )(x2d, p, p, p, ya, yf, yb, *consts)


ROUTE_TM = 512


def _route_body(h_ref, whi_ref, wlo_ref, b_ref, e_ref, g_ref, pos_ref, cnt_ref, carry):
    tm = ROUTE_TM

    @pl.when(pl.program_id(0) == 0)
    def _():
        carry[...] = jnp.zeros_like(carry)

    h = h_ref[...]
    hhi = h.astype(BF16)
    hlo = (h - hhi.astype(F32)).astype(BF16)
    nt = (((1,), (1,)), ((), ()))
    whi = whi_ref[...]
    logits = (lax.dot_general(whi, hhi, nt, preferred_element_type=F32)
              + lax.dot_general(whi, hlo, nt, preferred_element_type=F32)
              + lax.dot_general(wlo_ref[...], hhi, nt, preferred_element_type=F32)
              + b_ref[...])
    eiota = lax.broadcasted_iota(I32, (N_EXPERTS, tm), 0)
    cur = logits
    vals, hots = [], []
    for k in range(TOP_K):
        m = jnp.max(cur, axis=0, keepdims=True)
        idx = jnp.min(jnp.where(cur == m, eiota, N_EXPERTS), axis=0, keepdims=True)
        hot = eiota == idx
        e_ref[k:k + 1, :] = idx
        vals.append(m)
        hots.append(hot)
        cur = jnp.where(hot, -jnp.inf, cur)
    ex = [jnp.exp(v - vals[0]) for v in vals]
    den = ex[0] + ex[1] + ex[2] + ex[3]
    for k in range(TOP_K):
        g_ref[k:k + 1, :] = ex[k] / den
    chosen = jnp.zeros((N_EXPERTS, tm), F32)
    for hot in hots:
        chosen = chosen + hot.astype(F32)
    earlier = (lax.broadcasted_iota(I32, (tm, tm), 0) < lax.broadcasted_iota(I32, (tm, tm), 1)).astype(BF16)
    before = jnp.dot(chosen.astype(BF16), earlier, preferred_element_type=F32) + carry[:, 0:1]
    for k in range(TOP_K):
        pos_ref[k:k + 1, :] = jnp.sum(jnp.where(hots[k], before, 0.0), axis=0, keepdims=True).astype(I32)
    carry[...] = carry[...] + jnp.sum(chosen, axis=1, keepdims=True)
    cnt_ref[...] = carry[...].astype(I32)


def _route(h, w_hi, w_lo, b_col):
    t = h.shape[0]
    tm = ROUTE_TM
    kt = pl.BlockSpec((TOP_K, tm), lambda i: (0, i))
    full = lambda a: pl.BlockSpec(a.shape, lambda i: (0,) * a.ndim)
    return pl.pallas_call(
        _route_body,
        grid=(t // tm,),
        in_specs=[pl.BlockSpec((tm, D_MODEL), lambda i: (i, 0)), full(w_hi), full(w_lo), full(b_col)],
        out_specs=[kt, kt, kt, pl.BlockSpec((N_EXPERTS, LANES), lambda i: (0, 0))],
        out_shape=[jax.ShapeDtypeStruct((TOP_K, t), I32), jax.ShapeDtypeStruct((TOP_K, t), F32),
                   jax.ShapeDtypeStruct((TOP_K, t), I32), jax.ShapeDtypeStruct((N_EXPERTS, LANES), I32)],
        scratch_shapes=[pltpu.VMEM((N_EXPERTS, LANES), F32)],
        compiler_params=_cparams(("arbitrary",)),
        name="route",
    )(h, w_hi, w_lo, b_col)


SLOT_TM = 2048


def _slot_body(pstart_ref, e_ref, pos_ref, slot_ref):
    e = e_ref[...]
    base = jnp.zeros_like(e)
    for j in range(N_EXPERTS):
        base = jnp.where(e == j, pstart_ref[j], base)
    slot_ref[...] = base + pos_ref[...]


def _slots(pstart, eidx, pos):
    t = eidx.shape[1]
    kt = pl.BlockSpec((TOP_K, SLOT_TM), lambda i, ps: (0, i))
    return pl.pallas_call(
        _slot_body,
        grid_spec=pltpu.PrefetchScalarGridSpec(num_scalar_prefetch=1, grid=(t // SLOT_TM,),
                                               in_specs=[kt, kt], out_specs=kt),
        out_shape=jax.ShapeDtypeStruct((TOP_K, t), I32),
        compiler_params=_cparams(("parallel",)),
        name="slots",
    )(pstart, eidx, pos)


DISPATCH_TM = 512


def _dispatch_body(zfill_ref, slot_ref, h_ref, xs_ref, zbuf, sem_z, sem_r):
    i = pl.program_id(0)
    tm = DISPATCH_TM

    def zero_copy(e):
        return pltpu.make_async_copy(zbuf, xs_ref.at[pl.ds(zfill_ref[e], EXPERT_ROWS)], sem_z)

    @pl.when(i == 0)
    def _():
        zbuf[...] = jnp.zeros_like(zbuf)
        for e in range(N_EXPERTS):
            @pl.when(zfill_ref[e] >= 0)
            def _():
                zero_copy(e).start()
        for e in range(N_EXPERTS):
            @pl.when(zfill_ref[e] >= 0)
            def _():
                zero_copy(e).wait()

        nblocks = xs_ref.shape[0] // EXPERT_ROWS

        def tail_copy(b):
            return pltpu.make_async_copy(zbuf, xs_ref.at[pl.ds(b * EXPERT_ROWS, EXPERT_ROWS)], sem_z)

        def tail_start(b, c):
            tail_copy(b).start()
            return c

        def tail_wait(b, c):
            tail_copy(b).wait()
            return c

        lax.fori_loop(zfill_ref[N_EXPERTS], nblocks, tail_start, 0)
        lax.fori_loop(zfill_ref[N_EXPERTS], nblocks, tail_wait, 0)

    def token(tk, c):
        src = h_ref.at[tk]
        for k in range(TOP_K):
            pltpu.make_async_copy(src, xs_ref.at[slot_ref[k, tk]], sem_r).start(priority=k % 2)
        return c

    lax.fori_loop(0, tm, token, 0, unroll=ISSUE_UNROLL)
    done = xs_ref.at[pl.ds(0, TOP_K * tm)]
    pltpu.make_async_copy(done, done, sem_r).wait()


def _dispatch(zfill, slot, h, n_slots):
    t = h.shape[0]
    tm = DISPATCH_TM
    return pl.pallas_call(
        _dispatch_body,
        grid_spec=pltpu.PrefetchScalarGridSpec(
            num_scalar_prefetch=1, grid=(t // tm,),
            in_specs=[pl.BlockSpec((TOP_K, tm), lambda i, z: (0, i), memory_space=pltpu.SMEM),
                      pl.BlockSpec((tm, ROW_TILES, LANES), lambda i, z: (i, 0, 0))],
            out_specs=pl.BlockSpec(memory_space=pl.ANY),
            scratch_shapes=[pltpu.VMEM((EXPERT_ROWS, ROW_TILES, LANES), F32),
                            pltpu.SemaphoreType.DMA, pltpu.SemaphoreType.DMA]),
        out_shape=jax.ShapeDtypeStruct((n_slots, ROW_TILES, LANES), F32),
        compiler_params=_cparams(("arbitrary",)),
        name="dispatch",
    )(zfill, slot, h)


def _load_row_tiled(ref, first, nrows):
    return jnp.concatenate(
        [ref[pl.ds(first * ROW_TILES + c, nrows, stride=ROW_TILES), :] for c in range(ROW_TILES)], axis=1)


def _store_row_tiled(ref, val):
    nrows = val.shape[0]
    for c in range(ROW_TILES):
        ref[pl.ds(c, nrows, stride=ROW_TILES), :] = val[:, c * LANES:(c + 1) * LANES]


def _expert_body(be_ref, nused_ref, x_ref, wup_ref, bup_ref, wdn_ref, bdn_ref, o_ref):
    blk = pl.program_id(0)

    @pl.when(blk < nused_ref[0])
    def _():
        xb = _load_row_tiled(x_ref, 0, EXPERT_ROWS).astype(BF16)
        hu = jnp.dot(xb, wup_ref[...], preferred_element_type=F32) + bup_ref[...]
        x_glu = jnp.minimum(hu[:, :D_EXPERT], SWIGLU_LIMIT)
        x_lin = jnp.clip(hu[:, D_EXPERT:], -SWIGLU_LIMIT, SWIGLU_LIMIT)
        act = x_glu * jax.nn.sigmoid(SWIGLU_ALPHA * x_glu) * (x_lin + 1.0)
        out = jnp.dot(act.astype(BF16), wdn_ref[...], preferred_element_type=F32) + bdn_ref[...]
        _store_row_tiled(o_ref, out)

    @pl.when(blk >= nused_ref[0])
    def _():
        o_ref[...] = jnp.zeros_like(o_ref)


def _experts(block_expert, n_used, xs, w_up, b_up, w_down, b_down):
    n_slots = xs.shape[0]
    nb = n_slots // EXPERT_ROWS
    blk_rows = EXPERT_ROWS * ROW_TILES
    rows = lambda b, be, nu: (jnp.minimum(b, nu[0] - 1), 0)
    per_expert = lambda shape: pl.BlockSpec((None,) + shape, lambda b, be, nu: (be[b], 0, 0))
    out = pl.pallas_call(
        _expert_body,
        grid_spec=pltpu.PrefetchScalarGridSpec(
            num_scalar_prefetch=2, grid=(nb,),
            in_specs=[pl.BlockSpec((blk_rows, LANES), rows),
                      per_expert((D_MODEL, 2 * D_EXPERT)), per_expert((1, 2 * D_EXPERT)),
                      per_expert((D_EXPERT, D_MODEL)), per_expert((1, D_MODEL))],
            out_specs=pl.BlockSpec((blk_rows, LANES), lambda b, be, nu: (b, 0))),
        out_shape=jax.ShapeDtypeStruct((n_slots * ROW_TILES, LANES), F32),
        compiler_params=_cparams(("arbitrary",)),
        name="experts",
    )(block_expert, n_used, xs.reshape(n_slots * ROW_TILES, LANES), w_up, b_up, w_down, b_down)
    return out.reshape(n_slots, ROW_TILES, LANES)


COMBINE_TM = 256


def _combine_body(slot_ref, next_slot_ref, gate_ref, h_ref, out_ref, g_ref, b_ref, x_ref, gbuf, sem):
    tm = COMBINE_TM
    i = pl.program_id(0)
    cur = i % 2

    def gather(slots, b):
        def token(tk, c):
            for k in range(TOP_K):
                dst = gbuf.at[b, pl.ds(pl.multiple_of((k * tm + tk) * ROW_TILES, ROW_TILES), ROW_TILES)]
                pltpu.make_async_copy(out_ref.at[slots[k, tk]], dst, sem.at[b]).start(priority=k % 2)
            return c

        lax.fori_loop(0, tm, token, 0, unroll=ISSUE_UNROLL)

    @pl.when(i == 0)
    def _():
        gather(slot_ref, 0)

    @pl.when(i + 1 < pl.num_programs(0))
    def _():
        gather(next_slot_ref, 1 - cur)

    pltpu.make_async_copy(gbuf.at[cur], gbuf.at[cur], sem.at[cur]).wait()
    rows = gbuf.at[cur]
    gate = gate_ref[...]
    f = gate[:, 0:1] * _load_row_tiled(rows, 0, tm)
    for k in range(1, TOP_K):
        f = f + gate[:, k:k + 1] * _load_row_tiled(rows, k * tm, tm)
    x_ref[...] = _layer_norm(ALPHA * h_ref[...] + f, g_ref[...], b_ref[...])


def _combine(slot, gate_t, h, out, ln_g, ln_b):
    t = h.shape[0]
    tm = COMBINE_TM
    last = t // tm - 1
    return pl.pallas_call(
        _combine_body,
        grid=(t // tm,),
        in_specs=[pl.BlockSpec((TOP_K, tm), lambda i: (0, i), memory_space=pltpu.SMEM),
                  pl.BlockSpec((TOP_K, tm), lambda i: (0, jnp.minimum(i + 1, last)), memory_space=pltpu.SMEM),
                  pl.BlockSpec((tm, TOP_K), lambda i: (i, 0)),
                  pl.BlockSpec((tm, D_MODEL), lambda i: (i, 0)),
                  pl.BlockSpec(memory_space=pl.ANY),
                  pl.BlockSpec((1, D_MODEL), lambda i: (0, 0)),
                  pl.BlockSpec((1, D_MODEL), lambda i: (0, 0))],
        out_specs=pl.BlockSpec((tm, D_MODEL), lambda i: (i, 0)),
        out_shape=jax.ShapeDtypeStruct((t, D_MODEL), F32),
        scratch_shapes=[pltpu.VMEM((2, TOP_K * tm * ROW_TILES, LANES), F32), pltpu.SemaphoreType.DMA((2,))],
        compiler_params=_cparams(("arbitrary",)),
        name="combine",
    )(slot, slot, gate_t, h, out, ln_g, ln_b)


def _moe(h, h_rt, lw):
    t = h.shape[0]
    eidx, gate, pos, counts = _route(h, lw["wr_hi"], lw["wr_lo"], lw["br"])
    counts = counts[:, 0]
    padded = ((counts + EXPERT_ROWS - 1) // EXPERT_ROWS) * EXPERT_ROWS
    pend = jnp.cumsum(padded)
    pstart = pend - padded
    nb = (t * TOP_K) // EXPERT_ROWS + N_EXPERTS
    n_slots = nb * EXPERT_ROWS
    block_first = jnp.arange(nb, dtype=I32) * EXPERT_ROWS
    block_expert = jnp.minimum(jnp.sum((pend[None, :] <= block_first[:, None]).astype(I32), axis=1),
                               N_EXPERTS - 1).astype(I32)
    n_used = (pend[-1:] // EXPERT_ROWS).astype(I32)
    zfill = jnp.concatenate([jnp.where(padded > 0, pend - EXPERT_ROWS, -1).astype(I32), n_used])
    slot = _slots(pstart.astype(I32), eidx, pos)
    xs = _dispatch(zfill, slot, h_rt.reshape(t, ROW_TILES, LANES), n_slots)
    out = _experts(block_expert, n_used, xs, lw["w_up"], lw["b_up"], lw["w_down"], lw["b_down"])
    return _combine(slot, gate.T, h, out, lw["ln2_g"], lw["ln2_b"])


def _trunk(x, layers):
    batch, seq, _ = x.shape
    xf = x.reshape(batch * seq, D_MODEL)
    for lw in layers:
        p = _inproj(xf, lw["w_in"])
        ya = _natten(p, lw["att_bias"], batch, seq)
        yf, yb = _s5_scan(p, lw["s5"], batch, seq)
        h, h_rt = _mix(xf, p, ya, yf, yb, lw["d_skip"], lw["w_glu"], lw["b_glu"], lw["w_pa"], lw["w_pb"],
                       lw["w_out"], lw["ln1_g"], lw["ln1_b"])
        xf = _moe(h, h_rt, lw)
    return xf.reshape(batch, seq, D_MODEL)


def _prepare_layer(l, w_in, rpb, lam_re, lam_im, log_step, b_re, b_im, c_re, c_im, d_skip, w_glu, b_glu,
                   w_proj_a, w_proj_b, w_out, ln1_g, ln1_b, w_router, b_router, w_up, b_up, w_down,
                   b_down, ln2_g, ln2_b):
    wr_t = w_router[l].astype(F32).T
    wr_hi = wr_t.astype(BF16)
    wr_lo = (wr_t - wr_hi.astype(F32)).astype(BF16)
    row = lambda a: a.astype(F32).reshape(1, -1)
    return {
        "w_in": w_in[l].astype(BF16),
        "att_bias": _attention_bias(rpb[l]),
        "s5": _s5_weights(lam_re[l], lam_im[l], log_step[l], b_re[l], b_im[l], c_re[l], c_im[l]),
        "d_skip": row(d_skip[l]),
        "w_glu": w_glu[l].astype(BF16), "b_glu": row(b_glu[l]),
        "w_pa": w_proj_a[l].astype(BF16), "w_pb": w_proj_b[l].astype(BF16), "w_out": w_out[l].astype(BF16),
        "ln1_g": row(ln1_g[l]), "ln1_b": row(ln1_b[l]),
        "wr_hi": wr_hi, "wr_lo": wr_lo, "br": b_router[l].astype(F32).reshape(N_EXPERTS, 1),
        "w_up": w_up[l].astype(BF16), "b_up": b_up[l].astype(F32).reshape(N_EXPERTS, 1, 2 * D_EXPERT),
        "w_down": w_down[l].astype(BF16), "b_down": b_down[l].astype(F32).reshape(N_EXPERTS, 1, D_MODEL),
        "ln2_g": row(ln2_g[l]), "ln2_b": row(ln2_b[l]),
    }


def kernel(x_prompt, x_sample, w_in, rpb, lam_re, lam_im, log_step, b_re, b_im, c_re, c_im, d_skip, w_glu, b_glu, w_proj_a, w_proj_b, w_out, ln1_g, ln1_b, w_router, b_router, w_up, b_up, w_down, b_down, ln2_g, ln2_b):
    weights = (w_in, rpb, lam_re, lam_im, log_step, b_re, b_im, c_re, c_im, d_skip, w_glu, b_glu,
               w_proj_a, w_proj_b, w_out, ln1_g, ln1_b, w_router, b_router, w_up, b_up, w_down,
               b_down, ln2_g, ln2_b)
    layers = [_prepare_layer(l, *weights) for l in range(DEPTH)]
    return (_trunk(x_prompt, layers), _trunk(x_sample, layers))
```

```python
import functools
import math

import numpy as np
import jax
import jax.numpy as jnp
from jax import lax
from jax.experimental import pallas as pl
from jax.experimental.pallas import tpu as pltpu

F32 = jnp.float32
BF16 = jnp.bfloat16
I32 = jnp.int32

D_MODEL = 1024
DEPTH = 2
GRID_W = 64
WIN_H = 8
WIN_W = 16
ATT_HEADS = 8
HEAD_DIM = 64
ATT_WIDTH = ATT_HEADS * HEAD_DIM
SSM_GROUP = 16
SSM_GROUPS = 32
SSM_WIDTH = SSM_GROUP * SSM_GROUPS
SSM_STATE = 64
IN_WIDTH = 3 * ATT_WIDTH + SSM_WIDTH + 2 * D_MODEL
N_EXPERTS = 32
TOP_K = 4
D_EXPERT = 1024
SWIGLU_LIMIT = 7.0
SWIGLU_ALPHA = 1.702
ALPHA = (2 * DEPTH) ** 0.25
LN_EPS = 1e-5

LANES = 128
SUBLANES = 8
ROW_TILES = D_MODEL // LANES
VMEM_LIMIT = 56 * 1024 * 1024

HEAD_PAIRS = ATT_HEADS // 2
Q_ROWS = 16
BAND_ROWS = Q_ROWS + WIN_H
SLAB_GROUPS = LANES // SSM_GROUP
N_SLABS = SSM_GROUPS // SLAB_GROUPS
SLAB_STATE = SLAB_GROUPS * SSM_STATE
SCAN_TILE = 256
SCAN_ROWS = SUBLANES
EXPERT_ROWS = 512
ISSUE_UNROLL = 4
NEG_BIG = -1e30


def _cparams(sem):
    return pltpu.CompilerParams(dimension_semantics=sem, vmem_limit_bytes=VMEM_LIMIT)


def _layer_norm(z, g, b):
    mu = jnp.mean(z, axis=-1, keepdims=True)
    zc = z - mu
    var = jnp.mean(zc * zc, axis=-1, keepdims=True)
    return zc * lax.rsqrt(var + LN_EPS) * g + b


IN_TM = 512
IN_CHUNK = 1024


def _inproj_body(x_ref, w_ref, o_ref):
    xb = x_ref[...].astype(BF16)
    for c in range(IN_WIDTH // IN_CHUNK):
        sl = slice(c * IN_CHUNK, (c + 1) * IN_CHUNK)
        o_ref[:, sl] = jnp.dot(xb, w_ref[:, sl], preferred_element_type=F32).astype(BF16)


def _inproj(x2d, w):
    t = x2d.shape[0]
    return pl.pallas_call(
        _inproj_body,
        grid=(t // IN_TM,),
        in_specs=[pl.BlockSpec((IN_TM, D_MODEL), lambda i: (i, 0)),
                  pl.BlockSpec((D_MODEL, IN_WIDTH), lambda i: (0, 0))],
        out_specs=pl.BlockSpec((IN_TM, IN_WIDTH), lambda i: (i, 0)),
        out_shape=jax.ShapeDtypeStruct((t, IN_WIDTH), BF16),
        compiler_params=_cparams(("parallel",)),
        name="inproj",
    )(x2d, w)


def _attention_bias(rpb_l):
    cols = np.arange(GRID_W)
    col_start = np.clip(cols - WIN_W // 2, 0, GRID_W - WIN_W)
    kc = np.arange(GRID_W)
    inside = (kc[None, :] >= col_start[:, None]) & (kc[None, :] < col_start[:, None] + WIN_W)
    col_off = np.clip(kc[None, :] - cols[:, None] + (WIN_W - 1), 0, 2 * WIN_W - 2)
    delta = np.arange(WIN_H)
    j = np.arange(WIN_H)
    row_off = j[None, :] - delta[:, None] + (WIN_H - 1)
    b = rpb_l.astype(F32)[:, row_off]
    b = b[:, :, :, col_off]
    b = jnp.where(jnp.asarray(inside)[None, None, None], b, NEG_BIG)
    b = jnp.transpose(b, (0, 1, 3, 2, 4))
    b = b.reshape(HEAD_PAIRS, 2, WIN_H, GRID_W, WIN_H * GRID_W)
    b = jnp.transpose(b, (0, 2, 1, 3, 4))
    return b.reshape(HEAD_PAIRS, WIN_H, 2 * GRID_W, WIN_H * GRID_W)


def _natten_body(q_ref, k_ref, v_ref, bias_ref, o_ref, s_scr, p_scr, *, rows):
    r0 = pl.program_id(2) * Q_ROWS
    base = jnp.clip(r0 - WIN_H // 2, 0, rows - BAND_ROWS)
    lane = lax.broadcasted_iota(I32, (GRID_W, LANES), 1)
    first_head = lane < HEAD_DIM
    scale = HEAD_DIM ** -0.5
    nkeys = WIN_H * GRID_W
    nq = 2 * GRID_W

    def window(i):
        rs = jnp.clip(r0 + i - WIN_H // 2, 0, rows - WIN_H)
        return pl.multiple_of((rs - base) * GRID_W, GRID_W), r0 + i - rs

    for i in range(Q_ROWS):
        off, delta = window(i)
        qs = q_ref[i * GRID_W:(i + 1) * GRID_W, :] * scale
        zero = jnp.zeros_like(qs)
        qm = jnp.concatenate([jnp.where(first_head, qs, zero), jnp.where(first_head, zero, qs)], axis=0)
        s = lax.dot_general(qm, k_ref[pl.ds(off, nkeys), :], (((1,), (1,)), ((), ())),
                            preferred_element_type=F32)
        s_scr[i * nq:(i + 1) * nq, :] = s + bias_ref[delta]
    s = s_scr[...]
    e = jnp.exp(s - jnp.max(s, axis=1, keepdims=True))
    p_scr[...] = (e / jnp.sum(e, axis=1, keepdims=True)).astype(BF16)
    for i in range(Q_ROWS):
        off, _ = window(i)
        o = jnp.dot(p_scr[i * nq:(i + 1) * nq, :], v_ref[pl.ds(off, nkeys), :],
                    preferred_element_type=F32)
        oo = jnp.where(first_head, o[:GRID_W], o[GRID_W:])
        o_ref[i * GRID_W:(i + 1) * GRID_W, :] = oo.astype(BF16)


def _natten(p, bias, batch, seq):
    t = p.shape[0]
    rows = seq // GRID_W
    assert rows % Q_ROWS == 0 and rows >= BAND_ROWS
    nrb = rows // Q_ROWS
    qtok = Q_ROWS * GRID_W
    btok = BAND_ROWS * GRID_W
    kcol = ATT_WIDTH // LANES
    vcol = 2 * ATT_WIDTH // LANES

    def band_start(b, rb):
        base = jnp.clip(rb * Q_ROWS - WIN_H // 2, 0, rows - BAND_ROWS)
        return (b * rows + base) * GRID_W

    return pl.pallas_call(
        functools.partial(_natten_body, rows=rows),
        grid=(HEAD_PAIRS, batch, nrb),
        in_specs=[
            pl.BlockSpec((qtok, LANES), lambda pr, b, rb: (b * nrb + rb, pr)),
            pl.BlockSpec((pl.Element(btok), pl.Element(LANES)),
                         lambda pr, b, rb: (band_start(b, rb), (kcol + pr) * LANES)),
            pl.BlockSpec((pl.Element(btok), pl.Element(LANES)),
                         lambda pr, b, rb: (band_start(b, rb), (vcol + pr) * LANES)),
            pl.BlockSpec((None, WIN_H, 2 * GRID_W, WIN_H * GRID_W), lambda pr, b, rb: (pr, 0, 0, 0)),
        ],
        out_specs=pl.BlockSpec((qtok, LANES), lambda pr, b, rb: (b * nrb + rb, pr)),
        out_shape=jax.ShapeDtypeStruct((t, ATT_WIDTH), BF16),
        scratch_shapes=[pltpu.VMEM((Q_ROWS * 2 * GRID_W, WIN_H * GRID_W), F32),
                        pltpu.VMEM((Q_ROWS * 2 * GRID_W, WIN_H * GRID_W), BF16)],
        compiler_params=_cparams(("parallel", "parallel", "parallel")),
        name="natten",
    )(p, p, p, bias)


def _s5_weights(lam_re, lam_im, log_step, b_re, b_im, c_re, c_im):
    lr = lam_re.astype(F32)
    li = lam_im.astype(F32)
    step = jnp.exp(log_step.astype(F32))[..., None]
    mag = jnp.exp(lr * step)
    lb_re = mag * jnp.cos(li * step)
    lb_im = mag * jnp.sin(li * step)
    nr = lb_re - 1.0
    ni = lb_im
    den = lr * lr + li * li
    co_re = (nr * lr + ni * li) / den
    co_im = (ni * lr - nr * li) / den
    br_ = b_re.astype(F32)
    bi_ = b_im.astype(F32)
    bb_re = co_re[..., None] * br_ - co_im[..., None] * bi_
    bb_im = co_re[..., None] * bi_ + co_im[..., None] * br_
    eye = jnp.eye(SLAB_GROUPS, dtype=F32)

    def block_in(bb):
        bb = bb.reshape(2, N_SLABS, SLAB_GROUPS, SSM_STATE, SSM_GROUP)
        m = jnp.einsum('dsgph,gk->dsghkp', bb, eye)
        return m.reshape(2, N_SLABS, SLAB_GROUPS * SSM_GROUP, SLAB_STATE)

    def block_out(cc):
        cc = cc.reshape(2, N_SLABS, SLAB_GROUPS, SSM_GROUP, SSM_STATE)
        m = jnp.einsum('dsghp,gk->dsgpkh', cc, eye)
        return m.reshape(2, N_SLABS, SLAB_STATE, SLAB_GROUPS * SSM_GROUP)

    in_w = jnp.concatenate([block_in(bb_re), block_in(bb_im)], axis=-1).astype(BF16)
    out_w = jnp.concatenate([block_out(c_re.astype(F32)), -block_out(c_im.astype(F32))], axis=-2).astype(BF16)
    a_re = lb_re.reshape(2, N_SLABS, 1, SLAB_STATE)
    a_im = lb_im.reshape(2, N_SLABS, 1, SLAB_STATE)
    return in_w, out_w, a_re, a_im


def _s5_body(uf_ref, ub_ref, inw_ref, outw_ref, are_ref, aim_ref, yf_ref, yb_ref,
             uperm, buf, xsb, yperm, car_f, car_b, *, nseq, nslab):
    tt = SCAN_TILE
    ns = SLAB_STATE

    @pl.when(pl.program_id(2) == 0)
    def _():
        car_f[...] = jnp.zeros_like(car_f)
        car_b[...] = jnp.zeros_like(car_b)
        if nslab > 1:
            uperm[...] = jnp.zeros_like(uperm)

    for d, (u_ref, y_ref, car) in enumerate(((uf_ref, yf_ref, car_f), (ub_ref, yb_ref, car_b))):
        for q in range(nslab):
            for s in range(nseq):
                uperm[q, pl.ds(q * nseq + s, tt, stride=SCAN_ROWS), :] = (
                    u_ref[s, :, q * LANES:(q + 1) * LANES].astype(F32))
        lhs = jnp.concatenate([uperm[q] for q in range(nslab)], axis=1).astype(BF16)
        buf[...] = jnp.dot(lhs, inw_ref[d], preferred_element_type=F32)
        ar = are_ref[d]
        ai = aim_ref[d]

        def advance(c, row, ar=ar, ai=ai):
            xr, xi = c
            nr = ar * xr - ai * xi + buf[pl.ds(row, SCAN_ROWS), 0:ns]
            ni = ar * xi + ai * xr + buf[pl.ds(row, SCAN_ROWS), ns:2 * ns]
            return nr, ni

        def step_pair(i, c, d=d):
            lo = pl.multiple_of((2 * i if d == 0 else tt - 2 - 2 * i) * SCAN_ROWS, 2 * SCAN_ROWS)
            hi = lo + SCAN_ROWS
            first = advance(c, lo if d == 0 else hi)
            second = advance(first, hi if d == 0 else lo)
            x_lo, x_hi = (first, second) if d == 0 else (second, first)
            xsb[pl.ds(lo, 2 * SCAN_ROWS), 0:ns] = jnp.concatenate([x_lo[0], x_hi[0]], axis=0).astype(BF16)
            xsb[pl.ds(lo, 2 * SCAN_ROWS), ns:2 * ns] = jnp.concatenate([x_lo[1], x_hi[1]], axis=0).astype(BF16)
            return second

        xr, xi = lax.fori_loop(0, tt // 2, step_pair, (car[:, 0:ns], car[:, ns:2 * ns]), unroll=4)
        car[:, 0:ns] = xr
        car[:, ns:2 * ns] = xi
        y2 = jnp.dot(xsb[...], outw_ref[d], preferred_element_type=F32)
        for q in range(nslab):
            yperm[q] = y2[:, q * LANES:(q + 1) * LANES]
        for q in range(nslab):
            for s in range(nseq):
                y_ref[s, :, q * LANES:(q + 1) * LANES] = yperm[q, pl.ds(q * nseq + s, tt, stride=SCAN_ROWS), :]


def _s5_scan(p, s5w, batch, seq):
    in_w, out_w, a_re, a_im = s5w
    t = p.shape[0]
    nseq = min(batch, SUBLANES)
    ngroups = batch // nseq
    nslab = SCAN_ROWS // nseq
    assert nslab * nseq == SCAN_ROWS and N_SLABS % nslab == 0
    nstep = N_SLABS // nslab
    width = nslab * LANES
    nt = seq // SCAN_TILE
    ucol = 3 * ATT_WIDTH // width
    in_w = in_w.reshape(2, nstep, width, 2 * SLAB_STATE)
    out_w = jnp.transpose(out_w.reshape(2, nstep, nslab, 2 * SLAB_STATE, LANES), (0, 1, 3, 2, 4))
    out_w = out_w.reshape(2, nstep, 2 * SLAB_STATE, width)
    per_row = lambda a: jnp.repeat(a.reshape(2, nstep, nslab, SLAB_STATE), nseq, axis=2)
    a_re, a_im = per_row(a_re), per_row(a_im)
    p3 = p.reshape(batch, seq, IN_WIDTH)
    ublock = (nseq, SCAN_TILE, width)
    wspec = lambda shape: pl.BlockSpec((2, None) + shape, lambda g, s, j: (0, s, 0, 0))
    yf, yb = pl.pallas_call(
        functools.partial(_s5_body, nseq=nseq, nslab=nslab),
        grid=(ngroups, nstep, nt),
        in_specs=[
            pl.BlockSpec(ublock, lambda g, s, j: (g, j, ucol + s)),
            pl.BlockSpec(ublock, lambda g, s, j: (g, nt - 1 - j, ucol + s)),
            wspec((width, 2 * SLAB_STATE)),
            wspec((2 * SLAB_STATE, width)),
            wspec((SCAN_ROWS, SLAB_STATE)),
            wspec((SCAN_ROWS, SLAB_STATE)),
        ],
        out_specs=[pl.BlockSpec(ublock, lambda g, s, j: (g, j, s)),
                   pl.BlockSpec(ublock, lambda g, s, j: (g, nt - 1 - j, s))],
        out_shape=[jax.ShapeDtypeStruct((batch, seq, SSM_WIDTH), F32)] * 2,
        scratch_shapes=[pltpu.VMEM((nslab, SCAN_TILE * SCAN_ROWS, LANES), F32),
                        pltpu.VMEM((SCAN_TILE * SCAN_ROWS, 2 * SLAB_STATE), F32),
                        pltpu.VMEM((SCAN_TILE * SCAN_ROWS, 2 * SLAB_STATE), BF16),
                        pltpu.VMEM((nslab, SCAN_TILE * SCAN_ROWS, LANES), F32),
                        pltpu.VMEM((SCAN_ROWS, 2 * SLAB_STATE), F32),
                        pltpu.VMEM((SCAN_ROWS, 2 * SLAB_STATE), F32)],
        compiler_params=_cparams(("parallel", "parallel", "arbitrary")),
        name="s5_scan",
    )(p3, p3, in_w, out_w, a_re, a_im)
    return yf.reshape(t, SSM_WIDTH), yb.reshape(t, SSM_WIDTH)


MIX_TM = 256


def _mix_body(x_ref, u_ref, ga_ref, gb_ref, ya_ref, yf_ref, yb_ref, dsk_ref, wglu_ref, bglu_ref,
              wpa_ref, wpb_ref, wout_ref, g_ref, b_ref, h_ref, hrt_ref):
    y = dsk_ref[...] * u_ref[...].astype(F32) + yf_ref[...] + yb_ref[...]
    y = jax.nn.gelu(y)
    z = jnp.dot(y.astype(BF16), wglu_ref[...], preferred_element_type=F32) + bglu_ref[...]
    y = y * jax.nn.sigmoid(z)
    pa = jnp.dot(ya_ref[...], wpa_ref[...], preferred_element_type=F32)
    pb = jnp.dot(y.astype(BF16), wpb_ref[...], preferred_element_type=F32)
    mix = jax.nn.sigmoid(ga_ref[...].astype(F32)) * pa + jax.nn.sigmoid(gb_ref[...].astype(F32)) * pb
    m = jnp.dot(mix.astype(BF16), wout_ref[...], preferred_element_type=F32)
    h = _layer_norm(ALPHA * x_ref[...] + m, g_ref[...], b_ref[...])
    h_ref[...] = h
    _store_row_tiled(hrt_ref, h)


def _mix(x2d, p, ya, yf, yb, d_skip, w_glu, b_glu, w_pa, w_pb, w_out, ln_g, ln_b):
    t = x2d.shape[0]
    tm = MIX_TM
    row = lambda width, col=0: pl.BlockSpec((tm, width), lambda i: (i, col))
    full = lambda a: pl.BlockSpec(a.shape, lambda i: (0,) * a.ndim)
    consts = (d_skip, w_glu, b_glu, w_pa, w_pb, w_out, ln_g, ln_b)
    return pl.pallas_call(
        _mix_body,
        grid=(t // tm,),
        in_specs=[row(D_MODEL),
                  row(SSM_WIDTH, 3 * ATT_WIDTH // SSM_WIDTH),
                  row(D_MODEL, (3 * ATT_WIDTH + SSM_WIDTH) // D_MODEL),
                  row(D_MODEL, (3 * ATT_WIDTH + SSM_WIDTH) // D_MODEL + 1),
                  row(ATT_WIDTH), row(SSM_WIDTH), row(SSM_WIDTH)] + [full(a) for a in consts],
        out_specs=[row(D_MODEL), pl.BlockSpec((tm * ROW_TILES, LANES), lambda i: (i, 0))],
        out_shape=[jax.ShapeDtypeStruct((t, D_MODEL), F32),
                   jax.ShapeDtypeStruct((t * ROW_TILES, LANES), F32)],
        compiler_params=_cparams(("parallel",)),
        name="mix",
    )(x2d, p, p, p, ya, yf, yb, *consts)


ROUTE_TM = 512


def _route_body(h_ref, whi_ref, wlo_ref, b_ref, e_ref, g_ref, pos_ref, cnt_ref, carry):
    tm = ROUTE_TM

    @pl.when(pl.program_id(0) == 0)
    def _():
        carry[...] = jnp.zeros_like(carry)

    h = h_ref[...]
    hhi = h.astype(BF16)
    hlo = (h - hhi.astype(F32)).astype(BF16)
    nt = (((1,), (1,)), ((), ()))
    whi = whi_ref[...]
    logits = (lax.dot_general(whi, hhi, nt, preferred_element_type=F32)
              + lax.dot_general(whi, hlo, nt, preferred_element_type=F32)
              + lax.dot_general(wlo_ref[...], hhi, nt, preferred_element_type=F32)
              + b_ref[...])
    eiota = lax.broadcasted_iota(I32, (N_EXPERTS, tm), 0)
    cur = logits
    vals, hots = [], []
    for k in range(TOP_K):
        m = jnp.max(cur, axis=0, keepdims=True)
        idx = jnp.min(jnp.where(cur == m, eiota, N_EXPERTS), axis=0, keepdims=True)
        hot = eiota == idx
        e_ref[k:k + 1, :] = idx
        vals.append(m)
        hots.append(hot)
        cur = jnp.where(hot, -jnp.inf, cur)
    ex = [jnp.exp(v - vals[0]) for v in vals]
    den = ex[0] + ex[1] + ex[2] + ex[3]
    for k in range(TOP_K):
        g_ref[k:k + 1, :] = ex[k] / den
    chosen = jnp.zeros((N_EXPERTS, tm), F32)
    for hot in hots:
        chosen = chosen + hot.astype(F32)
    earlier = (lax.broadcasted_iota(I32, (tm, tm), 0) < lax.broadcasted_iota(I32, (tm, tm), 1)).astype(BF16)
    before = jnp.dot(chosen.astype(BF16), earlier, preferred_element_type=F32) + carry[:, 0:1]
    for k in range(TOP_K):
        pos_ref[k:k + 1, :] = jnp.sum(jnp.where(hots[k], before, 0.0), axis=0, keepdims=True).astype(I32)
    carry[...] = carry[...] + jnp.sum(chosen, axis=1, keepdims=True)
    cnt_ref[...] = carry[...].astype(I32)


def _route(h, w_hi, w_lo, b_col):
    t = h.shape[0]
    tm = ROUTE_TM
    kt = pl.BlockSpec((TOP_K, tm), lambda i: (0, i))
    full = lambda a: pl.BlockSpec(a.shape, lambda i: (0,) * a.ndim)
    return pl.pallas_call(
        _route_body,
        grid=(t // tm,),
        in_specs=[pl.BlockSpec((tm, D_MODEL), lambda i: (i, 0)), full(w_hi), full(w_lo), full(b_col)],
        out_specs=[kt, kt, kt, pl.BlockSpec((N_EXPERTS, LANES), lambda i: (0, 0))],
        out_shape=[jax.ShapeDtypeStruct((TOP_K, t), I32), jax.ShapeDtypeStruct((TOP_K, t), F32),
                   jax.ShapeDtypeStruct((TOP_K, t), I32), jax.ShapeDtypeStruct((N_EXPERTS, LANES), I32)],
        scratch_shapes=[pltpu.VMEM((N_EXPERTS, LANES), F32)],
        compiler_params=_cparams(("arbitrary",)),
        name="route",
    )(h, w_hi, w_lo, b_col)


SLOT_TM = 2048


def _slot_body(pstart_ref, e_ref, pos_ref, slot_ref):
    e = e_ref[...]
    base = jnp.zeros_like(e)
    for j in range(N_EXPERTS):
        base = jnp.where(e == j, pstart_ref[j], base)
    slot_ref[...] = base + pos_ref[...]


def _slots(pstart, eidx, pos):
    t = eidx.shape[1]
    kt = pl.BlockSpec((TOP_K, SLOT_TM), lambda i, ps: (0, i))
    return pl.pallas_call(
        _slot_body,
        grid_spec=pltpu.PrefetchScalarGridSpec(num_scalar_prefetch=1, grid=(t // SLOT_TM,),
                                               in_specs=[kt, kt], out_specs=kt),
        out_shape=jax.ShapeDtypeStruct((TOP_K, t), I32),
        compiler_params=_cparams(("parallel",)),
        name="slots",
    )(pstart, eidx, pos)


DISPATCH_TM = 512


def _dispatch_body(zfill_ref, slot_ref, h_ref, xs_ref, zbuf, sem_z, sem_r):
    i = pl.program_id(0)
    tm = DISPATCH_TM

    def zero_copy(e):
        return pltpu.make_async_copy(zbuf, xs_ref.at[pl.ds(zfill_ref[e], EXPERT_ROWS)], sem_z)

    @pl.when(i == 0)
    def _():
        zbuf[...] = jnp.zeros_like(zbuf)
        for e in range(N_EXPERTS):
            @pl.when(zfill_ref[e] >= 0)
            def _():
                zero_copy(e).start()
        for e in range(N_EXPERTS):
            @pl.when(zfill_ref[e] >= 0)
            def _():
                zero_copy(e).wait()

        nblocks = xs_ref.shape[0] // EXPERT_ROWS

        def tail_copy(b):
            return pltpu.make_async_copy(zbuf, xs_ref.at[pl.ds(b * EXPERT_ROWS, EXPERT_ROWS)], sem_z)

        def tail_start(b, c):
            tail_copy(b).start()
            return c

        def tail_wait(b, c):
            tail_copy(b).wait()
            return c

        lax.fori_loop(zfill_ref[N_EXPERTS], nblocks, tail_start, 0)
        lax.fori_loop(zfill_ref[N_EXPERTS], nblocks, tail_wait, 0)

    def token(tk, c):
        src = h_ref.at[tk]
        for k in range(TOP_K):
            pltpu.make_async_copy(src, xs_ref.at[slot_ref[k, tk]], sem_r).start(priority=k % 2)
        return c

    lax.fori_loop(0, tm, token, 0, unroll=ISSUE_UNROLL)
    done = xs_ref.at[pl.ds(0, TOP_K * tm)]
    pltpu.make_async_copy(done, done, sem_r).wait()


def _dispatch(zfill, slot, h, n_slots):
    t = h.shape[0]
    tm = DISPATCH_TM
    return pl.pallas_call(
        _dispatch_body,
        grid_spec=pltpu.PrefetchScalarGridSpec(
            num_scalar_prefetch=1, grid=(t // tm,),
            in_specs=[pl.BlockSpec((TOP_K, tm), lambda i, z: (0, i), memory_space=pltpu.SMEM),
                      pl.BlockSpec((tm, ROW_TILES, LANES), lambda i, z: (i, 0, 0))],
            out_specs=pl.BlockSpec(memory_space=pl.ANY),
            scratch_shapes=[pltpu.VMEM((EXPERT_ROWS, ROW_TILES, LANES), F32),
                            pltpu.SemaphoreType.DMA, pltpu.SemaphoreType.DMA]),
        out_shape=jax.ShapeDtypeStruct((n_slots, ROW_TILES, LANES), F32),
        compiler_params=_cparams(("arbitrary",)),
        name="dispatch",
    )(zfill, slot, h)


def _load_row_tiled(ref, first, nrows):
    return jnp.concatenate(
        [ref[pl.ds(first * ROW_TILES + c, nrows, stride=ROW_TILES), :] for c in range(ROW_TILES)], axis=1)


def _store_row_tiled(ref, val):
    nrows = val.shape[0]
    for c in range(ROW_TILES):
        ref[pl.ds(c, nrows, stride=ROW_TILES), :] = val[:, c * LANES:(c + 1) * LANES]


def _expert_body(be_ref, nused_ref, x_ref, wup_ref, bup_ref, wdn_ref, bdn_ref, o_ref):
    blk = pl.program_id(0)

    @pl.when(blk < nused_ref[0])
    def _():
        xb = _load_row_tiled(x_ref, 0, EXPERT_ROWS).astype(BF16)
        hu = jnp.dot(xb, wup_ref[...], preferred_element_type=F32) + bup_ref[...]
        x_glu = jnp.minimum(hu[:, :D_EXPERT], SWIGLU_LIMIT)
        x_lin = jnp.clip(hu[:, D_EXPERT:], -SWIGLU_LIMIT, SWIGLU_LIMIT)
        act = x_glu * jax.nn.sigmoid(SWIGLU_ALPHA * x_glu) * (x_lin + 1.0)
        out = jnp.dot(act.astype(BF16), wdn_ref[...], preferred_element_type=F32) + bdn_ref[...]
        _store_row_tiled(o_ref, out)

    @pl.when(blk >= nused_ref[0])
    def _():
        o_ref[...] = jnp.zeros_like(o_ref)


def _experts(block_expert, n_used, xs, w_up, b_up, w_down, b_down):
    n_slots = xs.shape[0]
    nb = n_slots // EXPERT_ROWS
    blk_rows = EXPERT_ROWS * ROW_TILES
    rows = lambda b, be, nu: (jnp.minimum(b, nu[0] - 1), 0)
    per_expert = lambda shape: pl.BlockSpec((None,) + shape, lambda b, be, nu: (be[b], 0, 0))
    out = pl.pallas_call(
        _expert_body,
        grid_spec=pltpu.PrefetchScalarGridSpec(
            num_scalar_prefetch=2, grid=(nb,),
            in_specs=[pl.BlockSpec((blk_rows, LANES), rows),
                      per_expert((D_MODEL, 2 * D_EXPERT)), per_expert((1, 2 * D_EXPERT)),
                      per_expert((D_EXPERT, D_MODEL)), per_expert((1, D_MODEL))],
            out_specs=pl.BlockSpec((blk_rows, LANES), lambda b, be, nu: (b, 0))),
        out_shape=jax.ShapeDtypeStruct((n_slots * ROW_TILES, LANES), F32),
        compiler_params=_cparams(("arbitrary",)),
        name="experts",
    )(block_expert, n_used, xs.reshape(n_slots * ROW_TILES, LANES), w_up, b_up, w_down, b_down)
    return out.reshape(n_slots, ROW_TILES, LANES)


COMBINE_TM = 256


def _combine_body(slot_ref, next_slot_ref, gate_ref, h_ref, out_ref, g_ref, b_ref, x_ref, gbuf, sem):
    tm = COMBINE_TM
    i = pl.program_id(0)
    cur = i % 2

    def gather(slots, b):
        def token(tk, c):
            for k in range(TOP_K):
                dst = gbuf.at[b, pl.ds(pl.multiple_of((k * tm + tk) * ROW_TILES, ROW_TILES), ROW_TILES)]
                pltpu.make_async_copy(out_ref.at[slots[k, tk]], dst, sem.at[b]).start(priority=k % 2)
            return c

        lax.fori_loop(0, tm, token, 0, unroll=ISSUE_UNROLL)

    @pl.when(i == 0)
    def _():
        gather(slot_ref, 0)

    @pl.when(i + 1 < pl.num_programs(0))
    def _():
        gather(next_slot_ref, 1 - cur)

    pltpu.make_async_copy(gbuf.at[cur], gbuf.at[cur], sem.at[cur]).wait()
    rows = gbuf.at[cur]
    gate = gate_ref[...]
    f = gate[:, 0:1] * _load_row_tiled(rows, 0, tm)
    for k in range(1, TOP_K):
        f = f + gate[:, k:k + 1] * _load_row_tiled(rows, k * tm, tm)
    x_ref[...] = _layer_norm(ALPHA * h_ref[...] + f, g_ref[...], b_ref[...])


def _combine(slot, gate_t, h, out, ln_g, ln_b):
    t = h.shape[0]
    tm = COMBINE_TM
    last = t // tm - 1
    return pl.pallas_call(
        _combine_body,
        grid=(t // tm,),
        in_specs=[pl.BlockSpec((TOP_K, tm), lambda i: (0, i), memory_space=pltpu.SMEM),
                  pl.BlockSpec((TOP_K, tm), lambda i: (0, jnp.minimum(i + 1, last)), memory_space=pltpu.SMEM),
                  pl.BlockSpec((tm, TOP_K), lambda i: (i, 0)),
                  pl.BlockSpec((tm, D_MODEL), lambda i: (i, 0)),
                  pl.BlockSpec(memory_space=pl.ANY),
                  pl.BlockSpec((1, D_MODEL), lambda i: (0, 0)),
                  pl.BlockSpec((1, D_MODEL), lambda i: (0, 0))],
        out_specs=pl.BlockSpec((tm, D_MODEL), lambda i: (i, 0)),
        out_shape=jax.ShapeDtypeStruct((t, D_MODEL), F32),
        scratch_shapes=[pltpu.VMEM((2, TOP_K * tm * ROW_TILES, LANES), F32), pltpu.SemaphoreType.DMA((2,))],
        compiler_params=_cparams(("arbitrary",)),
        name="combine",
    )(slot, slot, gate_t, h, out, ln_g, ln_b)


def _moe(h, h_rt, lw):
    t = h.shape[0]
    eidx, gate, pos, counts = _route(h, lw["wr_hi"], lw["wr_lo"], lw["br"])
    counts = counts[:, 0]
    padded = ((counts + EXPERT_ROWS - 1) // EXPERT_ROWS) * EXPERT_ROWS
    pend = jnp.cumsum(padded)
    pstart = pend - padded
    nb = (t * TOP_K) // EXPERT_ROWS + N_EXPERTS
    n_slots = nb * EXPERT_ROWS
    block_first = jnp.arange(nb, dtype=I32) * EXPERT_ROWS
    block_expert = jnp.minimum(jnp.sum((pend[None, :] <= block_first[:, None]).astype(I32), axis=1),
                               N_EXPERTS - 1).astype(I32)
    n_used = (pend[-1:] // EXPERT_ROWS).astype(I32)
    zfill = jnp.concatenate([jnp.where(padded > 0, pend - EXPERT_ROWS, -1).astype(I32), n_used])
    slot = _slots(pstart.astype(I32), eidx, pos)
    xs = _dispatch(zfill, slot, h_rt.reshape(t, ROW_TILES, LANES), n_slots)
    out = _experts(block_expert, n_used, xs, lw["w_up"], lw["b_up"], lw["w_down"], lw["b_down"])
    return _combine(slot, gate.T, h, out, lw["ln2_g"], lw["ln2_b"])


def _trunk(x, layers):
    batch, seq, _ = x.shape
    xf = x.reshape(batch * seq, D_MODEL)
    for lw in layers:
        p = _inproj(xf, lw["w_in"])
        ya = _natten(p, lw["att_bias"], batch, seq)
        yf, yb = _s5_scan(p, lw["s5"], batch, seq)
        h, h_rt = _mix(xf, p, ya, yf, yb, lw["d_skip"], lw["w_glu"], lw["b_glu"], lw["w_pa"], lw["w_pb"],
                       lw["w_out"], lw["ln1_g"], lw["ln1_b"])
        xf = _moe(h, h_rt, lw)
    return xf.reshape(batch, seq, D_MODEL)


def _prepare_layer(l, w_in, rpb, lam_re, lam_im, log_step, b_re, b_im, c_re, c_im, d_skip, w_glu, b_glu,
                   w_proj_a, w_proj_b, w_out, ln1_g, ln1_b, w_router, b_router, w_up, b_up, w_down,
                   b_down, ln2_g, ln2_b):
    wr_t = w_router[l].astype(F32).T
    wr_hi = wr_t.astype(BF16)
    wr_lo = (wr_t - wr_hi.astype(F32)).astype(BF16)
    row = lambda a: a.astype(F32).reshape(1, -1)
    return {
        "w_in": w_in[l].astype(BF16),
        "att_bias": _attention_bias(rpb[l]),
        "s5": _s5_weights(lam_re[l], lam_im[l], log_step[l], b_re[l], b_im[l], c_re[l], c_im[l]),
        "d_skip": row(d_skip[l]),
        "w_glu": w_glu[l].astype(BF16), "b_glu": row(b_glu[l]),
        "w_pa": w_proj_a[l].astype(BF16), "w_pb": w_proj_b[l].astype(BF16), "w_out": w_out[l].astype(BF16),
        "ln1_g": row(ln1_g[l]), "ln1_b": row(ln1_b[l]),
        "wr_hi": wr_hi, "wr_lo": wr_lo, "br": b_router[l].astype(F32).reshape(N_EXPERTS, 1),
        "w_up": w_up[l].astype(BF16), "b_up": b_up[l].astype(F32).reshape(N_EXPERTS, 1, 2 * D_EXPERT),
        "w_down": w_down[l].astype(BF16), "b_down": b_down[l].astype(F32).reshape(N_EXPERTS, 1, D_MODEL),
        "ln2_g": row(ln2_g[l]), "ln2_b": row(ln2_b[l]),
    }


def kernel(x_prompt, x_sample, w_in, rpb, lam_re, lam_im, log_step, b_re, b_im, c_re, c_im, d_skip, w_glu, b_glu, w_proj_a, w_proj_b, w_out, ln1_g, ln1_b, w_router, b_router, w_up, b_up, w_down, b_down, ln2_g, ln2_b):
    weights = (w_in, rpb, lam_re, lam_im, log_step, b_re, b_im, c_re, c_im, d_skip, w_glu, b_glu,
               w_proj_a, w_proj_b, w_out, ln1_g, ln1_b, w_router, b_router, w_up, b_up, w_down,
               b_down, ln2_g, ln2_b)
    layers = [_prepare_layer(l, *weights) for l in range(DEPTH)]
    return (_trunk(x_prompt, layers), _trunk(x_sample, layers))
```

```python
import functools
import math

import numpy as np
import jax
import jax.numpy as jnp
from jax import lax
from jax.experimental import pallas as pl
from jax.experimental.pallas import tpu as pltpu

F32 = jnp.float32
BF16 = jnp.bfloat16
I32 = jnp.int32

D_MODEL = 1024
DEPTH = 2
GRID_W = 64
WIN_H = 8
WIN_W = 16
ATT_HEADS = 8
HEAD_DIM = 64
ATT_WIDTH = ATT_HEADS * HEAD_DIM
SSM_GROUP = 16
SSM_GROUPS = 32
SSM_WIDTH = SSM_GROUP * SSM_GROUPS
SSM_STATE = 64
IN_WIDTH = 3 * ATT_WIDTH + SSM_WIDTH + 2 * D_MODEL
N_EXPERTS = 32
TOP_K = 4
D_EXPERT = 1024
SWIGLU_LIMIT = 7.0
SWIGLU_ALPHA = 1.702
ALPHA = (2 * DEPTH) ** 0.25
LN_EPS = 1e-5

LANES = 128
SUBLANES = 8
ROW_TILES = D_MODEL // LANES
VMEM_LIMIT = 56 * 1024 * 1024

HEAD_PAIRS = ATT_HEADS // 2
Q_ROWS = 16
BAND_ROWS = Q_ROWS + WIN_H
SLAB_GROUPS = LANES // SSM_GROUP
N_SLABS = SSM_GROUPS // SLAB_GROUPS
SLAB_STATE = SLAB_GROUPS * SSM_STATE
SCAN_TILE = 256
SCAN_ROWS = SUBLANES
EXPERT_ROWS = 512
ISSUE_UNROLL = 4
NEG_BIG = -1e30


def _cparams(sem):
    return pltpu.CompilerParams(dimension_semantics=sem, vmem_limit_bytes=VMEM_LIMIT)


def _layer_norm(z, g, b):
    mu = jnp.mean(z, axis=-1, keepdims=True)
    zc = z - mu
    var = jnp.mean(zc * zc, axis=-1, keepdims=True)
    return zc * lax.rsqrt(var + LN_EPS) * g + b


IN_TM = 512
IN_CHUNK = 1024


def _inproj_body(x_ref, w_ref, o_ref):
    xb = x_ref[...].astype(BF16)
    for c in range(IN_WIDTH // IN_CHUNK):
        sl = slice(c * IN_CHUNK, (c + 1) * IN_CHUNK)
        o_ref[:, sl] = jnp.dot(xb, w_ref[:, sl], preferred_element_type=F32).astype(BF16)


def _inproj(x2d, w):
    t = x2d.shape[0]
    return pl.pallas_call(
        _inproj_body,
        grid=(t // IN_TM,),
        in_specs=[pl.BlockSpec((IN_TM, D_MODEL), lambda i: (i, 0)),
                  pl.BlockSpec((D_MODEL, IN_WIDTH), lambda i: (0, 0))],
        out_specs=pl.BlockSpec((IN_TM, IN_WIDTH), lambda i: (i, 0)),
        out_shape=jax.ShapeDtypeStruct((t, IN_WIDTH), BF16),
        compiler_params=_cparams(("parallel",)),
        name="inproj",
    )(x2d, w)


def _attention_bias(rpb_l):
    cols = np.arange(GRID_W)
    col_start = np.clip(cols - WIN_W // 2, 0, GRID_W - WIN_W)
    kc = np.arange(GRID_W)
    inside = (kc[None, :] >= col_start[:, None]) & (kc[None, :] < col_start[:, None] + WIN_W)
    col_off = np.clip(kc[None, :] - cols[:, None] + (WIN_W - 1), 0, 2 * WIN_W - 2)
    delta = np.arange(WIN_H)
    j = np.arange(WIN_H)
    row_off = j[None, :] - delta[:, None] + (WIN_H - 1)
    b = rpb_l.astype(F32)[:, row_off]
    b = b[:, :, :, col_off]
    b = jnp.where(jnp.asarray(inside)[None, None, None], b, NEG_BIG)
    b = jnp.transpose(b, (0, 1, 3, 2, 4))
    b = b.reshape(HEAD_PAIRS, 2, WIN_H, GRID_W, WIN_H * GRID_W)
    b = jnp.transpose(b, (0, 2, 1, 3, 4))
    return b.reshape(HEAD_PAIRS, WIN_H, 2 * GRID_W, WIN_H * GRID_W)


def _natten_body(q_ref, k_ref, v_ref, bias_ref, o_ref, s_scr, p_scr, *, rows):
    r0 = pl.program_id(2) * Q_ROWS
    base = jnp.clip(r0 - WIN_H // 2, 0, rows - BAND_ROWS)
    lane = lax.broadcasted_iota(I32, (GRID_W, LANES), 1)
    first_head = lane < HEAD_DIM
    scale = HEAD_DIM ** -0.5
    nkeys = WIN_H * GRID_W
    nq = 2 * GRID_W

    def window(i):
        rs = jnp.clip(r0 + i - WIN_H // 2, 0, rows - WIN_H)
        return pl.multiple_of((rs - base) * GRID_W, GRID_W), r0 + i - rs

    for i in range(Q_ROWS):
        off, delta = window(i)
        qs = q_ref[i * GRID_W:(i + 1) * GRID_W, :] * scale
        zero = jnp.zeros_like(qs)
        qm = jnp.concatenate([jnp.where(first_head, qs, zero), jnp.where(first_head, zero, qs)], axis=0)
        s = lax.dot_general(qm, k_ref[pl.ds(off, nkeys), :], (((1,), (1,)), ((), ())),
                            preferred_element_type=F32)
        s_scr[i * nq:(i + 1) * nq, :] = s + bias_ref[delta]
    s = s_scr[...]
    e = jnp.exp(s - jnp.max(s, axis=1, keepdims=True))
    p_scr[...] = (e / jnp.sum(e, axis=1, keepdims=True)).astype(BF16)
    for i in range(Q_ROWS):
        off, _ = window(i)
        o = jnp.dot(p_scr[i * nq:(i + 1) * nq, :], v_ref[pl.ds(off, nkeys), :],
                    preferred_element_type=F32)
        oo = jnp.where(first_head, o[:GRID_W], o[GRID_W:])
        o_ref[i * GRID_W:(i + 1) * GRID_W, :] = oo.astype(BF16)


def _natten(p, bias, batch, seq):
    t = p.shape[0]
    rows = seq // GRID_W
    assert rows % Q_ROWS == 0 and rows >= BAND_ROWS
    nrb = rows // Q_ROWS
    qtok = Q_ROWS * GRID_W
    btok = BAND_ROWS * GRID_W
    kcol = ATT_WIDTH // LANES
    vcol = 2 * ATT_WIDTH // LANES

    def band_start(b, rb):
        base = jnp.clip(rb * Q_ROWS - WIN_H // 2, 0, rows - BAND_ROWS)
        return (b * rows + base) * GRID_W

    return pl.pallas_call(
        functools.partial(_natten_body, rows=rows),
        grid=(HEAD_PAIRS, batch, nrb),
        in_specs=[
            pl.BlockSpec((qtok, LANES), lambda pr, b, rb: (b * nrb + rb, pr)),
            pl.BlockSpec((pl.Element(btok), pl.Element(LANES)),
                         lambda pr, b, rb: (band_start(b, rb), (kcol + pr) * LANES)),
            pl.BlockSpec((pl.Element(btok), pl.Element(LANES)),
                         lambda pr, b, rb: (band_start(b, rb), (vcol + pr) * LANES)),
            pl.BlockSpec((None, WIN_H, 2 * GRID_W, WIN_H * GRID_W), lambda pr, b, rb: (pr, 0, 0, 0)),
        ],
        out_specs=pl.BlockSpec((qtok, LANES), lambda pr, b, rb: (b * nrb + rb, pr)),
        out_shape=jax.ShapeDtypeStruct((t, ATT_WIDTH), BF16),
        scratch_shapes=[pltpu.VMEM((Q_ROWS * 2 * GRID_W, WIN_H * GRID_W), F32),
                        pltpu.VMEM((Q_ROWS * 2 * GRID_W, WIN_H * GRID_W), BF16)],
        compiler_params=_cparams(("parallel", "parallel", "parallel")),
        name="natten",
    )(p, p, p, bias)


def _s5_weights(lam_re, lam_im, log_step, b_re, b_im, c_re, c_im):
    lr = lam_re.astype(F32)
    li = lam_im.astype(F32)
    step = jnp.exp(log_step.astype(F32))[..., None]
    mag = jnp.exp(lr * step)
    lb_re = mag * jnp.cos(li * step)
    lb_im = mag * jnp.sin(li * step)
    nr = lb_re - 1.0
    ni = lb_im
    den = lr * lr + li * li
    co_re = (nr * lr + ni * li) / den
    co_im = (ni * lr - nr * li) / den
    br_ = b_re.astype(F32)
    bi_ = b_im.astype(F32)
    bb_re = co_re[..., None] * br_ - co_im[..., None] * bi_
    bb_im = co_re[..., None] * bi_ + co_im[..., None] * br_
    eye = jnp.eye(SLAB_GROUPS, dtype=F32)

    def block_in(bb):
        bb = bb.reshape(2, N_SLABS, SLAB_GROUPS, SSM_STATE, SSM_GROUP)
        m = jnp.einsum('dsgph,gk->dsghkp', bb, eye)
        return m.reshape(2, N_SLABS, SLAB_GROUPS * SSM_GROUP, SLAB_STATE)

    def block_out(cc):
        cc = cc.reshape(2, N_SLABS, SLAB_GROUPS, SSM_GROUP, SSM_STATE)
        m = jnp.einsum('dsghp,gk->dsgpkh', cc, eye)
        return m.reshape(2, N_SLABS, SLAB_STATE, SLAB_GROUPS * SSM_GROUP)

    in_w = jnp.concatenate([block_in(bb_re), block_in(bb_im)], axis=-1).astype(BF16)
    out_w = jnp.concatenate([block_out(c_re.astype(F32)), -block_out(c_im.astype(F32))], axis=-2).astype(BF16)
    a_re = lb_re.reshape(2, N_SLABS, 1, SLAB_STATE)
    a_im = lb_im.reshape(2, N_SLABS, 1, SLAB_STATE)
    return in_w, out_w, a_re, a_im


def _s5_body(uf_ref, ub_ref, inw_ref, outw_ref, are_ref, aim_ref, yf_ref, yb_ref,
             uperm, buf, xsb, yperm, car_f, car_b, *, nseq, nslab):
    tt = SCAN_TILE
    ns = SLAB_STATE

    @pl.when(pl.program_id(2) == 0)
    def _():
        car_f[...] = jnp.zeros_like(car_f)
        car_b[...] = jnp.zeros_like(car_b)
        if nslab > 1:
            uperm[...] = jnp.zeros_like(uperm)

    for d, (u_ref, y_ref, car) in enumerate(((uf_ref, yf_ref, car_f), (ub_ref, yb_ref, car_b))):
        for q in range(nslab):
            for s in range(nseq):
                uperm[q, pl.ds(q * nseq + s, tt, stride=SCAN_ROWS), :] = (
                    u_ref[s, :, q * LANES:(q + 1) * LANES].astype(F32))
        lhs = jnp.concatenate([uperm[q] for q in range(nslab)], axis=1).astype(BF16)
        buf[...] = jnp.dot(lhs, inw_ref[d], preferred_element_type=F32)
        ar = are_ref[d]
        ai = aim_ref[d]

        def advance(c, row, ar=ar, ai=ai):
            xr, xi = c
            nr = ar * xr - ai * xi + buf[pl.ds(row, SCAN_ROWS), 0:ns]
            ni = ar * xi + ai * xr + buf[pl.ds(row, SCAN_ROWS), ns:2 * ns]
            return nr, ni

        def step_pair(i, c, d=d):
            lo = pl.multiple_of((2 * i if d == 0 else tt - 2 - 2 * i) * SCAN_ROWS, 2 * SCAN_ROWS)
            hi = lo + SCAN_ROWS
            first = advance(c, lo if d == 0 else hi)
            second = advance(first, hi if d == 0 else lo)
            x_lo, x_hi = (first, second) if d == 0 else (second, first)
            xsb[pl.ds(lo, 2 * SCAN_ROWS), 0:ns] = jnp.concatenate([x_lo[0], x_hi[0]], axis=0).astype(BF16)
            xsb[pl.ds(lo, 2 * SCAN_ROWS), ns:2 * ns] = jnp.concatenate([x_lo[1], x_hi[1]], axis=0).astype(BF16)
            return second

        xr, xi = lax.fori_loop(0, tt // 2, step_pair, (car[:, 0:ns], car[:, ns:2 * ns]), unroll=4)
        car[:, 0:ns] = xr
        car[:, ns:2 * ns] = xi
        y2 = jnp.dot(xsb[...], outw_ref[d], preferred_element_type=F32)
        for q in range(nslab):
            yperm[q] = y2[:, q * LANES:(q + 1) * LANES]
        for q in range(nslab):
            for s in range(nseq):
                y_ref[s, :, q * LANES:(q + 1) * LANES] = yperm[q, pl.ds(q * nseq + s, tt, stride=SCAN_ROWS), :]


def _s5_scan(p, s5w, batch, seq):
    in_w, out_w, a_re, a_im = s5w
    t = p.shape[0]
    nseq = min(batch, SUBLANES)
    ngroups = batch // nseq
    nslab = SCAN_ROWS // nseq
    assert nslab * nseq == SCAN_ROWS and N_SLABS % nslab == 0
    nstep = N_SLABS // nslab
    width = nslab * LANES
    nt = seq // SCAN_TILE
    ucol = 3 * ATT_WIDTH // width
    in_w = in_w.reshape(2, nstep, width, 2 * SLAB_STATE)
    out_w = jnp.transpose(out_w.reshape(2, nstep, nslab, 2 * SLAB_STATE, LANES), (0, 1, 3, 2, 4))
    out_w = out_w.reshape(2, nstep, 2 * SLAB_STATE, width)
    per_row = lambda a: jnp.repeat(a.reshape(2, nstep, nslab, SLAB_STATE), nseq, axis=2)
    a_re, a_im = per_row(a_re), per_row(a_im)
    p3 = p.reshape(batch, seq, IN_WIDTH)
    ublock = (nseq, SCAN_TILE, width)
    wspec = lambda shape: pl.BlockSpec((2, None) + shape, lambda g, s, j: (0, s, 0, 0))
    yf, yb = pl.pallas_call(
        functools.partial(_s5_body, nseq=nseq, nslab=nslab),
        grid=(ngroups, nstep, nt),
        in_specs=[
            pl.BlockSpec(ublock, lambda g, s, j: (g, j, ucol + s)),
            pl.BlockSpec(ublock, lambda g, s, j: (g, nt - 1 - j, ucol + s)),
            wspec((width, 2 * SLAB_STATE)),
            wspec((2 * SLAB_STATE, width)),
            wspec((SCAN_ROWS, SLAB_STATE)),
            wspec((SCAN_ROWS, SLAB_STATE)),
        ],
        out_specs=[pl.BlockSpec(ublock, lambda g, s, j: (g, j, s)),
                   pl.BlockSpec(ublock, lambda g, s, j: (g, nt - 1 - j, s))],
        out_shape=[jax.ShapeDtypeStruct((batch, seq, SSM_WIDTH), F32)] * 2,
        scratch_shapes=[pltpu.VMEM((nslab, SCAN_TILE * SCAN_ROWS, LANES), F32),
                        pltpu.VMEM((SCAN_TILE * SCAN_ROWS, 2 * SLAB_STATE), F32),
                        pltpu.VMEM((SCAN_TILE * SCAN_ROWS, 2 * SLAB_STATE), BF16),
                        pltpu.VMEM((nslab, SCAN_TILE * SCAN_ROWS, LANES), F32),
                        pltpu.VMEM((SCAN_ROWS, 2 * SLAB_STATE), F32),
                        pltpu.VMEM((SCAN_ROWS, 2 * SLAB_STATE), F32)],
        compiler_params=_cparams(("parallel", "parallel", "arbitrary")),
        name="s5_scan",
    )(p3, p3, in_w, out_w, a_re, a_im)
    return yf.reshape(t, SSM_WIDTH), yb.reshape(t, SSM_WIDTH)


MIX_TM = 256


def _mix_body(x_ref, u_ref, ga_ref, gb_ref, ya_ref, yf_ref, yb_ref, dsk_ref, wglu_ref, bglu_ref,
              wpa_ref, wpb_ref, wout_ref, g_ref, b_ref, h_ref, hrt_ref):
    y = dsk_ref[...] * u_ref[...].astype(F32) + yf_ref[...] + yb_ref[...]
    y = jax.nn.gelu(y)
    z = jnp.dot(y.astype(BF16), wglu_ref[...], preferred_element_type=F32) + bglu_ref[...]
    y = y * jax.nn.sigmoid(z)
    pa = jnp.dot(ya_ref[...], wpa_ref[...], preferred_element_type=F32)
    pb = jnp.dot(y.astype(BF16), wpb_ref[...], preferred_element_type=F32)
    mix = jax.nn.sigmoid(ga_ref[...].astype(F32)) * pa + jax.nn.sigmoid(gb_ref[...].astype(F32)) * pb
    m = jnp.dot(mix.astype(BF16), wout_ref[...], preferred_element_type=F32)
    h = _layer_norm(ALPHA * x_ref[...] + m, g_ref[...], b_ref[...])
    h_ref[...] = h
    _store_row_tiled(hrt_ref, h)


def _mix(x2d, p, ya, yf, yb, d_skip, w_glu, b_glu, w_pa, w_pb, w_out, ln_g, ln_b):
    t = x2d.shape[0]
    tm = MIX_TM
    row = lambda width, col=0: pl.BlockSpec((tm, width), lambda i: (i, col))
    full = lambda a: pl.BlockSpec(a.shape, lambda i: (0,) * a.ndim)
    consts = (d_skip, w_glu, b_glu, w_pa, w_pb, w_out, ln_g, ln_b)
    return pl.pallas_call(
        _mix_body,
        grid=(t // tm,),
        in_specs=[row(D_MODEL),
                  row(SSM_WIDTH, 3 * ATT_WIDTH // SSM_WIDTH),
                  row(D_MODEL, (3 * ATT_WIDTH + SSM_WIDTH) // D_MODEL),
                  row(D_MODEL, (3 * ATT_WIDTH + SSM_WIDTH) // D_MODEL + 1),
                  row(ATT_WIDTH), row(SSM_WIDTH), row(SSM_WIDTH)] + [full(a) for a in consts],
        out_specs=[row(D_MODEL), pl.BlockSpec((tm * ROW_TILES, LANES), lambda i: (i, 0))],
        out_shape=[jax.ShapeDtypeStruct((t, D_MODEL), F32),
                   jax.ShapeDtypeStruct((t * ROW_TILES, LANES), F32)],
        compiler_params=_cparams(("parallel",)),
        name="mix",
    )(x2d, p, p, p, ya, yf, yb, *consts)


ROUTE_TM = 512


def _route_body(h_ref, whi_ref, wlo_ref, b_ref, e_ref, g_ref, pos_ref, cnt_ref, carry):
    tm = ROUTE_TM

    @pl.when(pl.program_id(0) == 0)
    def _():
        carry[...] = jnp.zeros_like(carry)

    h = h_ref[...]
    hhi = h.astype(BF16)
    hlo = (h - hhi.astype(F32)).astype(BF16)
    nt = (((1,), (1,)), ((), ()))
    whi = whi_ref[...]
    logits = (lax.dot_general(whi, hhi, nt, preferred_element_type=F32)
              + lax.dot_general(whi, hlo, nt, preferred_element_type=F32)
              + lax.dot_general(wlo_ref[...], hhi, nt, preferred_element_type=F32)
              + b_ref[...])
    eiota = lax.broadcasted_iota(I32, (N_EXPERTS, tm), 0)
    cur = logits
    vals, hots = [], []
    for k in range(TOP_K):
        m = jnp.max(cur, axis=0, keepdims=True)
        idx = jnp.min(jnp.where(cur == m, eiota, N_EXPERTS), axis=0, keepdims=True)
        hot = eiota == idx
        e_ref[k:k + 1, :] = idx
        vals.append(m)
        hots.append(hot)
        cur = jnp.where(hot, -jnp.inf, cur)
    ex = [jnp.exp(v - vals[0]) for v in vals]
    den = ex[0] + ex[1] + ex[2] + ex[3]
    for k in range(TOP_K):
        g_ref[k:k + 1, :] = ex[k] / den
    chosen = jnp.zeros((N_EXPERTS, tm), F32)
    for hot in hots:
        chosen = chosen + hot.astype(F32)
    earlier = (lax.broadcasted_iota(I32, (tm, tm), 0) < lax.broadcasted_iota(I32, (tm, tm), 1)).astype(BF16)
    before = jnp.dot(chosen.astype(BF16), earlier, preferred_element_type=F32) + carry[:, 0:1]
    for k in range(TOP_K):
        pos_ref[k:k + 1, :] = jnp.sum(jnp.where(hots[k], before, 0.0), axis=0, keepdims=True).astype(I32)
    carry[...] = carry[...] + jnp.sum(chosen, axis=1, keepdims=True)
    cnt_ref[...] = carry[...].astype(I32)


def _route(h, w_hi, w_lo, b_col):
    t = h.shape[0]
    tm = ROUTE_TM
    kt = pl.BlockSpec((TOP_K, tm), lambda i: (0, i))
    full = lambda a: pl.BlockSpec(a.shape, lambda i: (0,) * a.ndim)
    return pl.pallas_call(
        _route_body,
        grid=(t // tm,),
        in_specs=[pl.BlockSpec((tm, D_MODEL), lambda i: (i, 0)), full(w_hi), full(w_lo), full(b_col)],
        out_specs=[kt, kt, kt, pl.BlockSpec((N_EXPERTS, LANES), lambda i: (0, 0))],
        out_shape=[jax.ShapeDtypeStruct((TOP_K, t), I32), jax.ShapeDtypeStruct((TOP_K, t), F32),
                   jax.ShapeDtypeStruct((TOP_K, t), I32), jax.ShapeDtypeStruct((N_EXPERTS, LANES), I32)],
        scratch_shapes=[pltpu.VMEM((N_EXPERTS, LANES), F32)],
        compiler_params=_cparams(("arbitrary",)),
        name="route",
    )(h, w_hi, w_lo, b_col)


SLOT_TM = 2048


def _slot_body(pstart_ref, e_ref, pos_ref, slot_ref):
    e = e_ref[...]
    base = jnp.zeros_like(e)
    for j in range(N_EXPERTS):
        base = jnp.where(e == j, pstart_ref[j], base)
    slot_ref[...] = base + pos_ref[...]


def _slots(pstart, eidx, pos):
    t = eidx.shape[1]
    kt = pl.BlockSpec((TOP_K, SLOT_TM), lambda i, ps: (0, i))
    return pl.pallas_call(
        _slot_body,
        grid_spec=pltpu.PrefetchScalarGridSpec(num_scalar_prefetch=1, grid=(t // SLOT_TM,),
                                               in_specs=[kt, kt], out_specs=kt),
        out_shape=jax.ShapeDtypeStruct((TOP_K, t), I32),
        compiler_params=_cparams(("parallel",)),
        name="slots",
    )(pstart, eidx, pos)


DISPATCH_TM = 512


def _dispatch_body(zfill_ref, slot_ref, h_ref, xs_ref, zbuf, sem_z, sem_r):
    i = pl.program_id(0)
    tm = DISPATCH_TM

    def zero_copy(e):
        return pltpu.make_async_copy(zbuf, xs_ref.at[pl.ds(zfill_ref[e], EXPERT_ROWS)], sem_z)

    @pl.when(i == 0)
    def _():
        zbuf[...] = jnp.zeros_like(zbuf)
        for e in range(N_EXPERTS):
            @pl.when(zfill_ref[e] >= 0)
            def _():
                zero_copy(e).start()
        for e in range(N_EXPERTS):
            @pl.when(zfill_ref[e] >= 0)
            def _():
                zero_copy(e).wait()

        nblocks = xs_ref.shape[0] // EXPERT_ROWS

        def tail_copy(b):
            return pltpu.make_async_copy(zbuf, xs_ref.at[pl.ds(b * EXPERT_ROWS, EXPERT_ROWS)], sem_z)

        def tail_start(b, c):
            tail_copy(b).start()
            return c

        def tail_wait(b, c):
            tail_copy(b).wait()
            return c

        lax.fori_loop(zfill_ref[N_EXPERTS], nblocks, tail_start, 0)
        lax.fori_loop(zfill_ref[N_EXPERTS], nblocks, tail_wait, 0)

    def token(tk, c):
        src = h_ref.at[tk]
        for k in range(TOP_K):
            pltpu.make_async_copy(src, xs_ref.at[slot_ref[k, tk]], sem_r).start(priority=k % 2)
        return c

    lax.fori_loop(0, tm, token, 0, unroll=ISSUE_UNROLL)
    done = xs_ref.at[pl.ds(0, TOP_K * tm)]
    pltpu.make_async_copy(done, done, sem_r).wait()


def _dispatch(zfill, slot, h, n_slots):
    t = h.shape[0]
    tm = DISPATCH_TM
    return pl.pallas_call(
        _dispatch_body,
        grid_spec=pltpu.PrefetchScalarGridSpec(
            num_scalar_prefetch=1, grid=(t // tm,),
            in_specs=[pl.BlockSpec((TOP_K, tm), lambda i, z: (0, i), memory_space=pltpu.SMEM),
                      pl.BlockSpec((tm, ROW_TILES, LANES), lambda i, z: (i, 0, 0))],
            out_specs=pl.BlockSpec(memory_space=pl.ANY),
            scratch_shapes=[pltpu.VMEM((EXPERT_ROWS, ROW_TILES, LANES), F32),
                            pltpu.SemaphoreType.DMA, pltpu.SemaphoreType.DMA]),
        out_shape=jax.ShapeDtypeStruct((n_slots, ROW_TILES, LANES), F32),
        compiler_params=_cparams(("arbitrary",)),
        name="dispatch",
    )(zfill, slot, h)


def _load_row_tiled(ref, first, nrows):
    return jnp.concatenate(
        [ref[pl.ds(first * ROW_TILES + c, nrows, stride=ROW_TILES), :] for c in range(ROW_TILES)], axis=1)


def _store_row_tiled(ref, val):
    nrows = val.shape[0]
    for c in range(ROW_TILES):
        ref[pl.ds(c, nrows, stride=ROW_TILES), :] = val[:, c * LANES:(c + 1) * LANES]


def _expert_body(be_ref, nused_ref, x_ref, wup_ref, bup_ref, wdn_ref, bdn_ref, o_ref, wup_bf, wdn_bf):
    blk = pl.program_id(0)
    used = blk < nused_ref[0]
    new_expert = (blk == 0) | (be_ref[blk] != be_ref[jnp.maximum(blk - 1, 0)])

    @pl.when(used & new_expert)
    def _():
        wup_bf[...] = wup_ref[...].astype(BF16)
        wdn_bf[...] = wdn_ref[...].astype(BF16)

    @pl.when(used)
    def _():
        xb = _load_row_tiled(x_ref, 0, EXPERT_ROWS).astype(BF16)
        hu = jnp.dot(xb, wup_bf[...], preferred_element_type=F32) + bup_ref[...]
        x_glu = jnp.minimum(hu[:, :D_EXPERT], SWIGLU_LIMIT)
        x_lin = jnp.clip(hu[:, D_EXPERT:], -SWIGLU_LIMIT, SWIGLU_LIMIT)
        act = x_glu * jax.nn.sigmoid(SWIGLU_ALPHA * x_glu) * (x_lin + 1.0)
        out = jnp.dot(act.astype(BF16), wdn_bf[...], preferred_element_type=F32) + bdn_ref[...]
        _store_row_tiled(o_ref, out)

    @pl.when(blk >= nused_ref[0])
    def _():
        o_ref[...] = jnp.zeros_like(o_ref)


def _experts(block_expert, n_used, xs, layer, w_up, b_up, w_down, b_down):
    n_slots = xs.shape[0]
    nb = n_slots // EXPERT_ROWS
    blk_rows = EXPERT_ROWS * ROW_TILES
    rows = lambda b, be, nu: (jnp.minimum(b, nu[0] - 1), 0)
    per_expert = lambda shape: pl.BlockSpec((None,) + shape, lambda b, be, nu: (be[b], 0, 0))
    stacked = lambda shape: pl.BlockSpec((None, None) + shape, lambda b, be, nu: (layer, be[b], 0, 0))
    out = pl.pallas_call(
        _expert_body,
        grid_spec=pltpu.PrefetchScalarGridSpec(
            num_scalar_prefetch=2, grid=(nb,),
            in_specs=[pl.BlockSpec((blk_rows, LANES), rows),
                      stacked((D_MODEL, 2 * D_EXPERT)), per_expert((1, 2 * D_EXPERT)),
                      stacked((D_EXPERT, D_MODEL)), per_expert((1, D_MODEL))],
            out_specs=pl.BlockSpec((blk_rows, LANES), lambda b, be, nu: (b, 0)),
            scratch_shapes=[pltpu.VMEM((D_MODEL, 2 * D_EXPERT), BF16), pltpu.VMEM((D_EXPERT, D_MODEL), BF16)]),
        out_shape=jax.ShapeDtypeStruct((n_slots * ROW_TILES, LANES), F32),
        compiler_params=_cparams(("arbitrary",)),
        name="experts",
    )(block_expert, n_used, xs.reshape(n_slots * ROW_TILES, LANES), w_up, b_up, w_down, b_down)
    return out.reshape(n_slots, ROW_TILES, LANES)


COMBINE_TM = 256


def _combine_body(slot_ref, next_slot_ref, gate_ref, h_ref, out_ref, g_ref, b_ref, x_ref, gbuf, sem):
    tm = COMBINE_TM
    i = pl.program_id(0)
    cur = i % 2

    def gather(slots, b):
        def token(tk, c):
            for k in range(TOP_K):
                dst = gbuf.at[b, pl.ds(pl.multiple_of((k * tm + tk) * ROW_TILES, ROW_TILES), ROW_TILES)]
                pltpu.make_async_copy(out_ref.at[slots[k, tk]], dst, sem.at[b]).start(priority=k % 2)
            return c

        lax.fori_loop(0, tm, token, 0, unroll=ISSUE_UNROLL)

    @pl.when(i == 0)
    def _():
        gather(slot_ref, 0)

    @pl.when(i + 1 < pl.num_programs(0))
    def _():
        gather(next_slot_ref, 1 - cur)

    pltpu.make_async_copy(gbuf.at[cur], gbuf.at[cur], sem.at[cur]).wait()
    rows = gbuf.at[cur]
    gate = gate_ref[...]
    f = gate[:, 0:1] * _load_row_tiled(rows, 0, tm)
    for k in range(1, TOP_K):
        f = f + gate[:, k:k + 1] * _load_row_tiled(rows, k * tm, tm)
    x_ref[...] = _layer_norm(ALPHA * h_ref[...] + f, g_ref[...], b_ref[...])


def _combine(slot, gate_t, h, out, ln_g, ln_b):
    t = h.shape[0]
    tm = COMBINE_TM
    last = t // tm - 1
    return pl.pallas_call(
        _combine_body,
        grid=(t // tm,),
        in_specs=[pl.BlockSpec((TOP_K, tm), lambda i: (0, i), memory_space=pltpu.SMEM),
                  pl.BlockSpec((TOP_K, tm), lambda i: (0, jnp.minimum(i + 1, last)), memory_space=pltpu.SMEM),
                  pl.BlockSpec((tm, TOP_K), lambda i: (i, 0)),
                  pl.BlockSpec((tm, D_MODEL), lambda i: (i, 0)),
                  pl.BlockSpec(memory_space=pl.ANY),
                  pl.BlockSpec((1, D_MODEL), lambda i: (0, 0)),
                  pl.BlockSpec((1, D_MODEL), lambda i: (0, 0))],
        out_specs=pl.BlockSpec((tm, D_MODEL), lambda i: (i, 0)),
        out_shape=jax.ShapeDtypeStruct((t, D_MODEL), F32),
        scratch_shapes=[pltpu.VMEM((2, TOP_K * tm * ROW_TILES, LANES), F32), pltpu.SemaphoreType.DMA((2,))],
        compiler_params=_cparams(("arbitrary",)),
        name="combine",
    )(slot, slot, gate_t, h, out, ln_g, ln_b)


def _moe(h, h_rt, lw):
    t = h.shape[0]
    eidx, gate, pos, counts = _route(h, lw["wr_hi"], lw["wr_lo"], lw["br"])
    counts = counts[:, 0]
    padded = ((counts + EXPERT_ROWS - 1) // EXPERT_ROWS) * EXPERT_ROWS
    pend = jnp.cumsum(padded)
    pstart = pend - padded
    nb = (t * TOP_K) // EXPERT_ROWS + N_EXPERTS
    n_slots = nb * EXPERT_ROWS
    block_first = jnp.arange(nb, dtype=I32) * EXPERT_ROWS
    block_expert = jnp.minimum(jnp.sum((pend[None, :] <= block_first[:, None]).astype(I32), axis=1),
                               N_EXPERTS - 1).astype(I32)
    n_used = (pend[-1:] // EXPERT_ROWS).astype(I32)
    zfill = jnp.concatenate([jnp.where(padded > 0, pend - EXPERT_ROWS, -1).astype(I32), n_used])
    slot = _slots(pstart.astype(I32), eidx, pos)
    xs = _dispatch(zfill, slot, h_rt.reshape(t, ROW_TILES, LANES), n_slots)
    out = _experts(block_expert, n_used, xs, lw["layer"], lw["w_up"], lw["b_up"], lw["w_down"], lw["b_down"])
    return _combine(slot, gate.T, h, out, lw["ln2_g"], lw["ln2_b"])


def _trunk(x, layers):
    batch, seq, _ = x.shape
    xf = x.reshape(batch * seq, D_MODEL)
    for lw in layers:
        p = _inproj(xf, lw["w_in"])
        ya = _natten(p, lw["att_bias"], batch, seq)
        yf, yb = _s5_scan(p, lw["s5"], batch, seq)
        h, h_rt = _mix(xf, p, ya, yf, yb, lw["d_skip"], lw["w_glu"], lw["b_glu"], lw["w_pa"], lw["w_pb"],
                       lw["w_out"], lw["ln1_g"], lw["ln1_b"])
        xf = _moe(h, h_rt, lw)
    return xf.reshape(batch, seq, D_MODEL)


def _prepare_layer(l, w_in, rpb, lam_re, lam_im, log_step, b_re, b_im, c_re, c_im, d_skip, w_glu, b_glu,
                   w_proj_a, w_proj_b, w_out, ln1_g, ln1_b, w_router, b_router, w_up, b_up, w_down,
                   b_down, ln2_g, ln2_b):
    wr_t = w_router[l].astype(F32).T
    wr_hi = wr_t.astype(BF16)
    wr_lo = (wr_t - wr_hi.astype(F32)).astype(BF16)
    row = lambda a: a.astype(F32).reshape(1, -1)
    return {
        "w_in": w_in[l].astype(BF16),
        "att_bias": _attention_bias(rpb[l]),
        "s5": _s5_weights(lam_re[l], lam_im[l], log_step[l], b_re[l], b_im[l], c_re[l], c_im[l]),
        "d_skip": row(d_skip[l]),
        "w_glu": w_glu[l].astype(BF16), "b_glu": row(b_glu[l]),
        "w_pa": w_proj_a[l].astype(BF16), "w_pb": w_proj_b[l].astype(BF16), "w_out": w_out[l].astype(BF16),
        "ln1_g": row(ln1_g[l]), "ln1_b": row(ln1_b[l]),
        "wr_hi": wr_hi, "wr_lo": wr_lo, "br": b_router[l].astype(F32).reshape(N_EXPERTS, 1),
        "layer": l,
        "w_up": w_up, "b_up": b_up[l].astype(F32).reshape(N_EXPERTS, 1, 2 * D_EXPERT),
        "w_down": w_down, "b_down": b_down[l].astype(F32).reshape(N_EXPERTS, 1, D_MODEL),
        "ln2_g": row(ln2_g[l]), "ln2_b": row(ln2_b[l]),
    }


def kernel(x_prompt, x_sample, w_in, rpb, lam_re, lam_im, log_step, b_re, b_im, c_re, c_im, d_skip, w_glu, b_glu, w_proj_a, w_proj_b, w_out, ln1_g, ln1_b, w_router, b_router, w_up, b_up, w_down, b_down, ln2_g, ln2_b):
    weights = (w_in, rpb, lam_re, lam_im, log_step, b_re, b_im, c_re, c_im, d_skip, w_glu, b_glu,
               w_proj_a, w_proj_b, w_out, ln1_g, ln1_b, w_router, b_router, w_up, b_up, w_down,
               b_down, ln2_g, ln2_b)
    layers = [_prepare_layer(l, *weights) for l in range(DEPTH)]
    return (_trunk(x_prompt, layers), _trunk(x_sample, layers))
```

```python
import functools
import math

import numpy as np
import jax
import jax.numpy as jnp
from jax import lax
from jax.experimental import pallas as pl
from jax.experimental.pallas import tpu as pltpu

F32 = jnp.float32
BF16 = jnp.bfloat16
I32 = jnp.int32

D_MODEL = 1024
DEPTH = 2
GRID_W = 64
WIN_H = 8
WIN_W = 16
ATT_HEADS = 8
HEAD_DIM = 64
ATT_WIDTH = ATT_HEADS * HEAD_DIM
SSM_GROUP = 16
SSM_GROUPS = 32
SSM_WIDTH = SSM_GROUP * SSM_GROUPS
SSM_STATE = 64
IN_WIDTH = 3 * ATT_WIDTH + SSM_WIDTH + 2 * D_MODEL
N_EXPERTS = 32
TOP_K = 4
D_EXPERT = 1024
SWIGLU_LIMIT = 7.0
SWIGLU_ALPHA = 1.702
ALPHA = (2 * DEPTH) ** 0.25
LN_EPS = 1e-5

LANES = 128
SUBLANES = 8
ROW_TILES = D_MODEL // LANES
VMEM_LIMIT = 56 * 1024 * 1024

HEAD_PAIRS = ATT_HEADS // 2
Q_ROWS = 16
BAND_ROWS = Q_ROWS + WIN_H
SLAB_GROUPS = LANES // SSM_GROUP
N_SLABS = SSM_GROUPS // SLAB_GROUPS
SLAB_STATE = SLAB_GROUPS * SSM_STATE
SCAN_TILE = 256
SCAN_ROWS = SUBLANES
EXPERT_ROWS = 512
ISSUE_UNROLL = 4
NEG_BIG = -1e30


def _cparams(sem):
    return pltpu.CompilerParams(dimension_semantics=sem, vmem_limit_bytes=VMEM_LIMIT)


def _layer_norm(z, g, b):
    mu = jnp.mean(z, axis=-1, keepdims=True)
    zc = z - mu
    var = jnp.mean(zc * zc, axis=-1, keepdims=True)
    return zc * lax.rsqrt(var + LN_EPS) * g + b


IN_TM = 512
IN_CHUNK = 1024


def _inproj_body(x_ref, w_ref, o_ref):
    xb = x_ref[...].astype(BF16)
    for c in range(IN_WIDTH // IN_CHUNK):
        sl = slice(c * IN_CHUNK, (c + 1) * IN_CHUNK)
        o_ref[:, sl] = jnp.dot(xb, w_ref[:, sl], preferred_element_type=F32).astype(BF16)


def _inproj(x2d, w):
    t = x2d.shape[0]
    return pl.pallas_call(
        _inproj_body,
        grid=(t // IN_TM,),
        in_specs=[pl.BlockSpec((IN_TM, D_MODEL), lambda i: (i, 0)),
                  pl.BlockSpec((D_MODEL, IN_WIDTH), lambda i: (0, 0))],
        out_specs=pl.BlockSpec((IN_TM, IN_WIDTH), lambda i: (i, 0)),
        out_shape=jax.ShapeDtypeStruct((t, IN_WIDTH), BF16),
        compiler_params=_cparams(("parallel",)),
        name="inproj",
    )(x2d, w)


def _attention_bias(rpb_l):
    cols = np.arange(GRID_W)
    col_start = np.clip(cols - WIN_W // 2, 0, GRID_W - WIN_W)
    kc = np.arange(GRID_W)
    inside = (kc[None, :] >= col_start[:, None]) & (kc[None, :] < col_start[:, None] + WIN_W)
    col_off = np.clip(kc[None, :] - cols[:, None] + (WIN_W - 1), 0, 2 * WIN_W - 2)
    delta = np.arange(WIN_H)
    j = np.arange(WIN_H)
    row_off = j[None, :] - delta[:, None] + (WIN_H - 1)
    b = rpb_l.astype(F32)[:, row_off]
    b = b[:, :, :, col_off]
    b = jnp.where(jnp.asarray(inside)[None, None, None], b, NEG_BIG)
    b = jnp.transpose(b, (0, 1, 3, 2, 4))
    b = b.reshape(HEAD_PAIRS, 2, WIN_H, GRID_W, WIN_H * GRID_W)
    b = jnp.transpose(b, (0, 2, 1, 3, 4))
    return b.reshape(HEAD_PAIRS, WIN_H, 2 * GRID_W, WIN_H * GRID_W)


def _natten_body(q_ref, k_ref, v_ref, bias_ref, o_ref, s_scr, p_scr, *, rows):
    r0 = pl.program_id(2) * Q_ROWS
    base = jnp.clip(r0 - WIN_H // 2, 0, rows - BAND_ROWS)
    lane = lax.broadcasted_iota(I32, (GRID_W, LANES), 1)
    first_head = lane < HEAD_DIM
    scale = HEAD_DIM ** -0.5
    nkeys = WIN_H * GRID_W
    nq = 2 * GRID_W

    def window(i):
        rs = jnp.clip(r0 + i - WIN_H // 2, 0, rows - WIN_H)
        return pl.multiple_of((rs - base) * GRID_W, GRID_W), r0 + i - rs

    for i in range(Q_ROWS):
        off, delta = window(i)
        qs = q_ref[i * GRID_W:(i + 1) * GRID_W, :] * scale
        zero = jnp.zeros_like(qs)
        qm = jnp.concatenate([jnp.where(first_head, qs, zero), jnp.where(first_head, zero, qs)], axis=0)
        s = lax.dot_general(qm, k_ref[pl.ds(off, nkeys), :], (((1,), (1,)), ((), ())),
                            preferred_element_type=F32)
        s_scr[i * nq:(i + 1) * nq, :] = s + bias_ref[delta]
    s = s_scr[...]
    e = jnp.exp(s - jnp.max(s, axis=1, keepdims=True))
    p_scr[...] = (e / jnp.sum(e, axis=1, keepdims=True)).astype(BF16)
    for i in range(Q_ROWS):
        off, _ = window(i)
        o = jnp.dot(p_scr[i * nq:(i + 1) * nq, :], v_ref[pl.ds(off, nkeys), :],
                    preferred_element_type=F32)
        oo = jnp.where(first_head, o[:GRID_W], o[GRID_W:])
        o_ref[i * GRID_W:(i + 1) * GRID_W, :] = oo.astype(BF16)


def _natten(p, bias, batch, seq):
    t = p.shape[0]
    rows = seq // GRID_W
    assert rows % Q_ROWS == 0 and rows >= BAND_ROWS
    nrb = rows // Q_ROWS
    qtok = Q_ROWS * GRID_W
    btok = BAND_ROWS * GRID_W
    kcol = ATT_WIDTH // LANES
    vcol = 2 * ATT_WIDTH // LANES

    def band_start(b, rb):
        base = jnp.clip(rb * Q_ROWS - WIN_H // 2, 0, rows - BAND_ROWS)
        return (b * rows + base) * GRID_W

    return pl.pallas_call(
        functools.partial(_natten_body, rows=rows),
        grid=(HEAD_PAIRS, batch, nrb),
        in_specs=[
            pl.BlockSpec((qtok, LANES), lambda pr, b, rb: (b * nrb + rb, pr)),
            pl.BlockSpec((pl.Element(btok), pl.Element(LANES)),
                         lambda pr, b, rb: (band_start(b, rb), (kcol + pr) * LANES)),
            pl.BlockSpec((pl.Element(btok), pl.Element(LANES)),
                         lambda pr, b, rb: (band_start(b, rb), (vcol + pr) * LANES)),
            pl.BlockSpec((None, WIN_H, 2 * GRID_W, WIN_H * GRID_W), lambda pr, b, rb: (pr, 0, 0, 0)),
        ],
        out_specs=pl.BlockSpec((qtok, LANES), lambda pr, b, rb: (b * nrb + rb, pr)),
        out_shape=jax.ShapeDtypeStruct((t, ATT_WIDTH), BF16),
        scratch_shapes=[pltpu.VMEM((Q_ROWS * 2 * GRID_W, WIN_H * GRID_W), F32),
                        pltpu.VMEM((Q_ROWS * 2 * GRID_W, WIN_H * GRID_W), BF16)],
        compiler_params=_cparams(("parallel", "parallel", "parallel")),
        name="natten",
    )(p, p, p, bias)


def _s5_weights(lam_re, lam_im, log_step, b_re, b_im, c_re, c_im):
    lr = lam_re.astype(F32)
    li = lam_im.astype(F32)
    step = jnp.exp(log_step.astype(F32))[..., None]
    mag = jnp.exp(lr * step)
    lb_re = mag * jnp.cos(li * step)
    lb_im = mag * jnp.sin(li * step)
    nr = lb_re - 1.0
    ni = lb_im
    den = lr * lr + li * li
    co_re = (nr * lr + ni * li) / den
    co_im = (ni * lr - nr * li) / den
    br_ = b_re.astype(F32)
    bi_ = b_im.astype(F32)
    bb_re = co_re[..., None] * br_ - co_im[..., None] * bi_
    bb_im = co_re[..., None] * bi_ + co_im[..., None] * br_
    eye = jnp.eye(SLAB_GROUPS, dtype=F32)

    def block_in(bb):
        bb = bb.reshape(2, N_SLABS, SLAB_GROUPS, SSM_STATE, SSM_GROUP)
        m = jnp.einsum('dsgph,gk->dsghkp', bb, eye)
        return m.reshape(2, N_SLABS, SLAB_GROUPS * SSM_GROUP, SLAB_STATE)

    def block_out(cc):
        cc = cc.reshape(2, N_SLABS, SLAB_GROUPS, SSM_GROUP, SSM_STATE)
        m = jnp.einsum('dsghp,gk->dsgpkh', cc, eye)
        return m.reshape(2, N_SLABS, SLAB_STATE, SLAB_GROUPS * SSM_GROUP)

    in_w = jnp.concatenate([block_in(bb_re), block_in(bb_im)], axis=-1).astype(BF16)
    out_w = jnp.concatenate([block_out(c_re.astype(F32)), -block_out(c_im.astype(F32))], axis=-2).astype(BF16)
    a_re = lb_re.reshape(2, N_SLABS, 1, SLAB_STATE)
    a_im = lb_im.reshape(2, N_SLABS, 1, SLAB_STATE)
    return in_w, out_w, a_re, a_im


def _s5_body(uf_ref, ub_ref, inw_ref, outw_ref, are_ref, aim_ref, yf_ref, yb_ref,
             uperm, buf, xsb, yperm, car_f, car_b, *, nseq, nslab):
    tt = SCAN_TILE
    ns = SLAB_STATE

    @pl.when(pl.program_id(2) == 0)
    def _():
        car_f[...] = jnp.zeros_like(car_f)
        car_b[...] = jnp.zeros_like(car_b)
        if nslab > 1:
            uperm[...] = jnp.zeros_like(uperm)

    for d, (u_ref, y_ref, car) in enumerate(((uf_ref, yf_ref, car_f), (ub_ref, yb_ref, car_b))):
        for q in range(nslab):
            for s in range(nseq):
                uperm[q, pl.ds(q * nseq + s, tt, stride=SCAN_ROWS), :] = (
                    u_ref[s, :, q * LANES:(q + 1) * LANES].astype(F32))
        lhs = jnp.concatenate([uperm[q] for q in range(nslab)], axis=1).astype(BF16)
        buf[...] = jnp.dot(lhs, inw_ref[d], preferred_element_type=F32)
        ar = are_ref[d]
        ai = aim_ref[d]

        def advance(c, row, ar=ar, ai=ai):
            xr, xi = c
            nr = ar * xr - ai * xi + buf[pl.ds(row, SCAN_ROWS), 0:ns]
            ni = ar * xi + ai * xr + buf[pl.ds(row, SCAN_ROWS), ns:2 * ns]
            return nr, ni

        def step_pair(i, c, d=d):
            lo = pl.multiple_of((2 * i if d == 0 else tt - 2 - 2 * i) * SCAN_ROWS, 2 * SCAN_ROWS)
            hi = lo + SCAN_ROWS
            first = advance(c, lo if d == 0 else hi)
            second = advance(first, hi if d == 0 else lo)
            x_lo, x_hi = (first, second) if d == 0 else (second, first)
            xsb[pl.ds(lo, 2 * SCAN_ROWS), 0:ns] = jnp.concatenate([x_lo[0], x_hi[0]], axis=0).astype(BF16)
            xsb[pl.ds(lo, 2 * SCAN_ROWS), ns:2 * ns] = jnp.concatenate([x_lo[1], x_hi[1]], axis=0).astype(BF16)
            return second

        xr, xi = lax.fori_loop(0, tt // 2, step_pair, (car[:, 0:ns], car[:, ns:2 * ns]), unroll=4)
        car[:, 0:ns] = xr
        car[:, ns:2 * ns] = xi
        y2 = jnp.dot(xsb[...], outw_ref[d], preferred_element_type=F32)
        for q in range(nslab):
            yperm[q] = y2[:, q * LANES:(q + 1) * LANES]
        for q in range(nslab):
            for s in range(nseq):
                y_ref[s, :, q * LANES:(q + 1) * LANES] = yperm[q, pl.ds(q * nseq + s, tt, stride=SCAN_ROWS), :]


def _s5_scan(p, s5w, batch, seq):
    in_w, out_w, a_re, a_im = s5w
    t = p.shape[0]
    nseq = min(batch, SUBLANES)
    ngroups = batch // nseq
    nslab = SCAN_ROWS // nseq
    assert nslab * nseq == SCAN_ROWS and N_SLABS % nslab == 0
    nstep = N_SLABS // nslab
    width = nslab * LANES
    nt = seq // SCAN_TILE
    ucol = 3 * ATT_WIDTH // width
    in_w = in_w.reshape(2, nstep, width, 2 * SLAB_STATE)
    out_w = jnp.transpose(out_w.reshape(2, nstep, nslab, 2 * SLAB_STATE, LANES), (0, 1, 3, 2, 4))
    out_w = out_w.reshape(2, nstep, 2 * SLAB_STATE, width)
    per_row = lambda a: jnp.repeat(a.reshape(2, nstep, nslab, SLAB_STATE), nseq, axis=2)
    a_re, a_im = per_row(a_re), per_row(a_im)
    p3 = p.reshape(batch, seq, IN_WIDTH)
    ublock = (nseq, SCAN_TILE, width)
    wspec = lambda shape: pl.BlockSpec((2, None) + shape, lambda g, s, j: (0, s, 0, 0))
    yf, yb = pl.pallas_call(
        functools.partial(_s5_body, nseq=nseq, nslab=nslab),
        grid=(ngroups, nstep, nt),
        in_specs=[
            pl.BlockSpec(ublock, lambda g, s, j: (g, j, ucol + s)),
            pl.BlockSpec(ublock, lambda g, s, j: (g, nt - 1 - j, ucol + s)),
            wspec((width, 2 * SLAB_STATE)),
            wspec((2 * SLAB_STATE, width)),
            wspec((SCAN_ROWS, SLAB_STATE)),
            wspec((SCAN_ROWS, SLAB_STATE)),
        ],
        out_specs=[pl.BlockSpec(ublock, lambda g, s, j: (g, j, s)),
                   pl.BlockSpec(ublock, lambda g, s, j: (g, nt - 1 - j, s))],
        out_shape=[jax.ShapeDtypeStruct((batch, seq, SSM_WIDTH), F32)] * 2,
        scratch_shapes=[pltpu.VMEM((nslab, SCAN_TILE * SCAN_ROWS, LANES), F32),
                        pltpu.VMEM((SCAN_TILE * SCAN_ROWS, 2 * SLAB_STATE), F32),
                        pltpu.VMEM((SCAN_TILE * SCAN_ROWS, 2 * SLAB_STATE), BF16),
                        pltpu.VMEM((nslab, SCAN_TILE * SCAN_ROWS, LANES), F32),
                        pltpu.VMEM((SCAN_ROWS, 2 * SLAB_STATE), F32),
                        pltpu.VMEM((SCAN_ROWS, 2 * SLAB_STATE), F32)],
        compiler_params=_cparams(("parallel", "parallel", "arbitrary")),
        name="s5_scan",
    )(p3, p3, in_w, out_w, a_re, a_im)
    return yf.reshape(t, SSM_WIDTH), yb.reshape(t, SSM_WIDTH)


MIX_TM = 256


def _mix_body(x_ref, u_ref, ga_ref, gb_ref, ya_ref, yf_ref, yb_ref, dsk_ref, wglu_ref, bglu_ref,
              wpa_ref, wpb_ref, wout_ref, g_ref, b_ref, whi_ref, wlo_ref, br_ref,
              hrt_ref, e_ref, gate_ref, pos_ref, cnt_ref, carry):
    y = dsk_ref[...] * u_ref[...].astype(F32) + yf_ref[...] + yb_ref[...]
    y = jax.nn.gelu(y)
    z = jnp.dot(y.astype(BF16), wglu_ref[...], preferred_element_type=F32) + bglu_ref[...]
    y = y * jax.nn.sigmoid(z)
    pa = jnp.dot(ya_ref[...], wpa_ref[...], preferred_element_type=F32)
    pb = jnp.dot(y.astype(BF16), wpb_ref[...], preferred_element_type=F32)
    mix = jax.nn.sigmoid(ga_ref[...].astype(F32)) * pa + jax.nn.sigmoid(gb_ref[...].astype(F32)) * pb
    m = jnp.dot(mix.astype(BF16), wout_ref[...], preferred_element_type=F32)
    h = _layer_norm(ALPHA * x_ref[...] + m, g_ref[...], b_ref[...])
    _store_row_tiled(hrt_ref, h)
    _route_tile(h, whi_ref, wlo_ref, br_ref, e_ref, gate_ref, pos_ref, cnt_ref, carry)


def _mix(x2d, p, ya, yf, yb, d_skip, w_glu, b_glu, w_pa, w_pb, w_out, ln_g, ln_b, wr_hi, wr_lo, br):
    t = x2d.shape[0]
    tm = MIX_TM
    row = lambda width, col=0: pl.BlockSpec((tm, width), lambda i: (i, col))
    full = lambda a: pl.BlockSpec(a.shape, lambda i: (0,) * a.ndim)
    kt = pl.BlockSpec((TOP_K, tm), lambda i: (0, i))
    consts = (d_skip, w_glu, b_glu, w_pa, w_pb, w_out, ln_g, ln_b, wr_hi, wr_lo, br)
    return pl.pallas_call(
        _mix_body,
        grid=(t // tm,),
        in_specs=[row(D_MODEL),
                  row(SSM_WIDTH, 3 * ATT_WIDTH // SSM_WIDTH),
                  row(D_MODEL, (3 * ATT_WIDTH + SSM_WIDTH) // D_MODEL),
                  row(D_MODEL, (3 * ATT_WIDTH + SSM_WIDTH) // D_MODEL + 1),
                  row(ATT_WIDTH), row(SSM_WIDTH), row(SSM_WIDTH)] + [full(a) for a in consts],
        out_specs=[pl.BlockSpec((tm * ROW_TILES, LANES), lambda i: (i, 0)), kt, kt, kt,
                   pl.BlockSpec((N_EXPERTS, LANES), lambda i: (0, 0))],
        out_shape=[jax.ShapeDtypeStruct((t * ROW_TILES, LANES), F32),
                   jax.ShapeDtypeStruct((TOP_K, t), I32), jax.ShapeDtypeStruct((TOP_K, t), F32),
                   jax.ShapeDtypeStruct((TOP_K, t), I32), jax.ShapeDtypeStruct((N_EXPERTS, LANES), I32)],
        scratch_shapes=[pltpu.VMEM((N_EXPERTS, LANES), F32)],
        compiler_params=_cparams(("arbitrary",)),
        name="mix",
    )(x2d, p, p, p, ya, yf, yb, *consts)


def _route_tile(h, whi_ref, wlo_ref, b_ref, e_ref, g_ref, pos_ref, cnt_ref, carry):
    tm = h.shape[0]

    @pl.when(pl.program_id(0) == 0)
    def _():
        carry[...] = jnp.zeros_like(carry)

    hhi = h.astype(BF16)
    hlo = (h - hhi.astype(F32)).astype(BF16)
    nt = (((1,), (1,)), ((), ()))
    whi = whi_ref[...]
    logits = (lax.dot_general(whi, hhi, nt, preferred_element_type=F32)
              + lax.dot_general(whi, hlo, nt, preferred_element_type=F32)
              + lax.dot_general(wlo_ref[...], hhi, nt, preferred_element_type=F32)
              + b_ref[...])
    eiota = lax.broadcasted_iota(I32, (N_EXPERTS, tm), 0)
    cur = logits
    vals, hots = [], []
    for k in range(TOP_K):
        m = jnp.max(cur, axis=0, keepdims=True)
        idx = jnp.min(jnp.where(cur == m, eiota, N_EXPERTS), axis=0, keepdims=True)
        hot = eiota == idx
        e_ref[k:k + 1, :] = idx
        vals.append(m)
        hots.append(hot)
        cur = jnp.where(hot, -jnp.inf, cur)
    ex = [jnp.exp(v - vals[0]) for v in vals]
    den = ex[0] + ex[1] + ex[2] + ex[3]
    for k in range(TOP_K):
        g_ref[k:k + 1, :] = ex[k] / den
    chosen = jnp.zeros((N_EXPERTS, tm), F32)
    for hot in hots:
        chosen = chosen + hot.astype(F32)
    earlier = (lax.broadcasted_iota(I32, (tm, tm), 0) < lax.broadcasted_iota(I32, (tm, tm), 1)).astype(BF16)
    before = jnp.dot(chosen.astype(BF16), earlier, preferred_element_type=F32) + carry[:, 0:1]
    for k in range(TOP_K):
        pos_ref[k:k + 1, :] = jnp.sum(jnp.where(hots[k], before, 0.0), axis=0, keepdims=True).astype(I32)
    carry[...] = carry[...] + jnp.sum(chosen, axis=1, keepdims=True)
    cnt_ref[...] = carry[...].astype(I32)


SLOT_TM = 2048


def _slot_body(pstart_ref, e_ref, pos_ref, slot_ref):
    e = e_ref[...]
    base = jnp.zeros_like(e)
    for j in range(N_EXPERTS):
        base = jnp.where(e == j, pstart_ref[j], base)
    slot_ref[...] = base + pos_ref[...]


def _slots(pstart, eidx, pos):
    t = eidx.shape[1]
    kt = pl.BlockSpec((TOP_K, SLOT_TM), lambda i, ps: (0, i))
    return pl.pallas_call(
        _slot_body,
        grid_spec=pltpu.PrefetchScalarGridSpec(num_scalar_prefetch=1, grid=(t // SLOT_TM,),
                                               in_specs=[kt, kt], out_specs=kt),
        out_shape=jax.ShapeDtypeStruct((TOP_K, t), I32),
        compiler_params=_cparams(("parallel",)),
        name="slots",
    )(pstart, eidx, pos)


DISPATCH_TM = 512


def _dispatch_body(zfill_ref, slot_ref, h_ref, xs_ref, zbuf, sem_z, sem_r):
    i = pl.program_id(0)
    tm = DISPATCH_TM

    def zero_copy(e):
        return pltpu.make_async_copy(zbuf, xs_ref.at[pl.ds(zfill_ref[e], EXPERT_ROWS)], sem_z)

    @pl.when(i == 0)
    def _():
        zbuf[...] = jnp.zeros_like(zbuf)
        for e in range(N_EXPERTS):
            @pl.when(zfill_ref[e] >= 0)
            def _():
                zero_copy(e).start()
        for e in range(N_EXPERTS):
            @pl.when(zfill_ref[e] >= 0)
            def _():
                zero_copy(e).wait()

        nblocks = xs_ref.shape[0] // EXPERT_ROWS

        def tail_copy(b):
            return pltpu.make_async_copy(zbuf, xs_ref.at[pl.ds(b * EXPERT_ROWS, EXPERT_ROWS)], sem_z)

        def tail_start(b, c):
            tail_copy(b).start()
            return c

        def tail_wait(b, c):
            tail_copy(b).wait()
            return c

        lax.fori_loop(zfill_ref[N_EXPERTS], nblocks, tail_start, 0)
        lax.fori_loop(zfill_ref[N_EXPERTS], nblocks, tail_wait, 0)

    def token(tk, c):
        src = h_ref.at[tk]
        for k in range(TOP_K):
            pltpu.make_async_copy(src, xs_ref.at[slot_ref[k, tk]], sem_r).start(priority=k % 2)
        return c

    lax.fori_loop(0, tm, token, 0, unroll=ISSUE_UNROLL)
    done = xs_ref.at[pl.ds(0, TOP_K * tm)]
    pltpu.make_async_copy(done, done, sem_r).wait()


def _dispatch(zfill, slot, h, n_slots):
    t = h.shape[0]
    tm = DISPATCH_TM
    return pl.pallas_call(
        _dispatch_body,
        grid_spec=pltpu.PrefetchScalarGridSpec(
            num_scalar_prefetch=1, grid=(t // tm,),
            in_specs=[pl.BlockSpec((TOP_K, tm), lambda i, z: (0, i), memory_space=pltpu.SMEM),
                      pl.BlockSpec((tm, ROW_TILES, LANES), lambda i, z: (i, 0, 0))],
            out_specs=pl.BlockSpec(memory_space=pl.ANY),
            scratch_shapes=[pltpu.VMEM((EXPERT_ROWS, ROW_TILES, LANES), F32),
                            pltpu.SemaphoreType.DMA, pltpu.SemaphoreType.DMA]),
        out_shape=jax.ShapeDtypeStruct((n_slots, ROW_TILES, LANES), F32),
        compiler_params=_cparams(("arbitrary",)),
        name="dispatch",
    )(zfill, slot, h)


def _load_row_tiled(ref, first, nrows):
    return jnp.concatenate(
        [ref[pl.ds(first * ROW_TILES + c, nrows, stride=ROW_TILES), :] for c in range(ROW_TILES)], axis=1)


def _store_row_tiled(ref, val):
    nrows = val.shape[0]
    for c in range(ROW_TILES):
        ref[pl.ds(c, nrows, stride=ROW_TILES), :] = val[:, c * LANES:(c + 1) * LANES]


def _expert_body(be_ref, nused_ref, x_ref, wup_ref, bup_ref, wdn_ref, bdn_ref, o_ref, wup_bf, wdn_bf):
    blk = pl.program_id(0)
    used = blk < nused_ref[0]
    new_expert = (blk == 0) | (be_ref[blk] != be_ref[jnp.maximum(blk - 1, 0)])

    @pl.when(used & new_expert)
    def _():
        wup_bf[...] = wup_ref[...].astype(BF16)
        wdn_bf[...] = wdn_ref[...].astype(BF16)

    @pl.when(used)
    def _():
        xb = _load_row_tiled(x_ref, 0, EXPERT_ROWS).astype(BF16)
        hu = jnp.dot(xb, wup_bf[...], preferred_element_type=F32) + bup_ref[...]
        x_glu = jnp.minimum(hu[:, :D_EXPERT], SWIGLU_LIMIT)
        x_lin = jnp.clip(hu[:, D_EXPERT:], -SWIGLU_LIMIT, SWIGLU_LIMIT)
        act = x_glu * jax.nn.sigmoid(SWIGLU_ALPHA * x_glu) * (x_lin + 1.0)
        out = jnp.dot(act.astype(BF16), wdn_bf[...], preferred_element_type=F32) + bdn_ref[...]
        _store_row_tiled(o_ref, out)

    @pl.when(blk >= nused_ref[0])
    def _():
        o_ref[...] = jnp.zeros_like(o_ref)


def _experts(block_expert, n_used, xs, layer, w_up, b_up, w_down, b_down):
    n_slots = xs.shape[0]
    nb = n_slots // EXPERT_ROWS
    blk_rows = EXPERT_ROWS * ROW_TILES
    rows = lambda b, be, nu: (jnp.minimum(b, nu[0] - 1), 0)
    per_expert = lambda shape: pl.BlockSpec((None,) + shape, lambda b, be, nu: (be[b], 0, 0))
    stacked = lambda shape: pl.BlockSpec((None, None) + shape, lambda b, be, nu: (layer, be[b], 0, 0))
    out = pl.pallas_call(
        _expert_body,
        grid_spec=pltpu.PrefetchScalarGridSpec(
            num_scalar_prefetch=2, grid=(nb,),
            in_specs=[pl.BlockSpec((blk_rows, LANES), rows),
                      stacked((D_MODEL, 2 * D_EXPERT)), per_expert((1, 2 * D_EXPERT)),
                      stacked((D_EXPERT, D_MODEL)), per_expert((1, D_MODEL))],
            out_specs=pl.BlockSpec((blk_rows, LANES), lambda b, be, nu: (b, 0)),
            scratch_shapes=[pltpu.VMEM((D_MODEL, 2 * D_EXPERT), BF16), pltpu.VMEM((D_EXPERT, D_MODEL), BF16)]),
        out_shape=jax.ShapeDtypeStruct((n_slots * ROW_TILES, LANES), F32),
        compiler_params=_cparams(("arbitrary",)),
        name="experts",
    )(block_expert, n_used, xs.reshape(n_slots * ROW_TILES, LANES), w_up, b_up, w_down, b_down)
    return out.reshape(n_slots, ROW_TILES, LANES)


COMBINE_TM = 256


def _combine_body(slot_ref, next_slot_ref, gate_ref, h_ref, out_ref, g_ref, b_ref, x_ref, gbuf, sem):
    tm = COMBINE_TM
    i = pl.program_id(0)
    cur = i % 2

    def gather(slots, b):
        def token(tk, c):
            for k in range(TOP_K):
                dst = gbuf.at[b, pl.ds(pl.multiple_of((k * tm + tk) * ROW_TILES, ROW_TILES), ROW_TILES)]
                pltpu.make_async_copy(out_ref.at[slots[k, tk]], dst, sem.at[b]).start(priority=k % 2)
            return c

        lax.fori_loop(0, tm, token, 0, unroll=ISSUE_UNROLL)

    @pl.when(i == 0)
    def _():
        gather(slot_ref, 0)

    @pl.when(i + 1 < pl.num_programs(0))
    def _():
        gather(next_slot_ref, 1 - cur)

    pltpu.make_async_copy(gbuf.at[cur], gbuf.at[cur], sem.at[cur]).wait()
    rows = gbuf.at[cur]
    gate = gate_ref[...]
    f = gate[:, 0:1] * _load_row_tiled(rows, 0, tm)
    for k in range(1, TOP_K):
        f = f + gate[:, k:k + 1] * _load_row_tiled(rows, k * tm, tm)
    x_ref[...] = _layer_norm(ALPHA * _load_row_tiled(h_ref, 0, tm) + f, g_ref[...], b_ref[...])


def _combine(slot, gate_t, h_rt, out, ln_g, ln_b):
    t = h_rt.shape[0] // ROW_TILES
    tm = COMBINE_TM
    last = t // tm - 1
    return pl.pallas_call(
        _combine_body,
        grid=(t // tm,),
        in_specs=[pl.BlockSpec((TOP_K, tm), lambda i: (0, i), memory_space=pltpu.SMEM),
                  pl.BlockSpec((TOP_K, tm), lambda i: (0, jnp.minimum(i + 1, last)), memory_space=pltpu.SMEM),
                  pl.BlockSpec((tm, TOP_K), lambda i: (i, 0)),
                  pl.BlockSpec((tm * ROW_TILES, LANES), lambda i: (i, 0)),
                  pl.BlockSpec(memory_space=pl.ANY),
                  pl.BlockSpec((1, D_MODEL), lambda i: (0, 0)),
                  pl.BlockSpec((1, D_MODEL), lambda i: (0, 0))],
        out_specs=pl.BlockSpec((tm, D_MODEL), lambda i: (i, 0)),
        out_shape=jax.ShapeDtypeStruct((t, D_MODEL), F32),
        scratch_shapes=[pltpu.VMEM((2, TOP_K * tm * ROW_TILES, LANES), F32), pltpu.SemaphoreType.DMA((2,))],
        compiler_params=_cparams(("arbitrary",)),
        name="combine",
    )(slot, slot, gate_t, h_rt, out, ln_g, ln_b)


def _moe(h_rt, eidx, gate, pos, counts, lw):
    t = h_rt.shape[0] // ROW_TILES
    counts = counts[:, 0]
    padded = ((counts + EXPERT_ROWS - 1) // EXPERT_ROWS) * EXPERT_ROWS
    pend = jnp.cumsum(padded)
    pstart = pend - padded
    nb = (t * TOP_K) // EXPERT_ROWS + N_EXPERTS
    n_slots = nb * EXPERT_ROWS
    block_first = jnp.arange(nb, dtype=I32) * EXPERT_ROWS
    block_expert = jnp.minimum(jnp.sum((pend[None, :] <= block_first[:, None]).astype(I32), axis=1),
                               N_EXPERTS - 1).astype(I32)
    n_used = (pend[-1:] // EXPERT_ROWS).astype(I32)
    zfill = jnp.concatenate([jnp.where(padded > 0, pend - EXPERT_ROWS, -1).astype(I32), n_used])
    slot = _slots(pstart.astype(I32), eidx, pos)
    xs = _dispatch(zfill, slot, h_rt.reshape(t, ROW_TILES, LANES), n_slots)
    out = _experts(block_expert, n_used, xs, lw["layer"], lw["w_up"], lw["b_up"], lw["w_down"], lw["b_down"])
    return _combine(slot, gate.T, h_rt, out, lw["ln2_g"], lw["ln2_b"])


def _trunk(x, layers):
    batch, seq, _ = x.shape
    xf = x.reshape(batch * seq, D_MODEL)
    for lw in layers:
        p = _inproj(xf, lw["w_in"])
        ya = _natten(p, lw["att_bias"], batch, seq)
        yf, yb = _s5_scan(p, lw["s5"], batch, seq)
        routed = _mix(xf, p, ya, yf, yb, lw["d_skip"], lw["w_glu"], lw["b_glu"], lw["w_pa"], lw["w_pb"],
                      lw["w_out"], lw["ln1_g"], lw["ln1_b"], lw["wr_hi"], lw["wr_lo"], lw["br"])
        xf = _moe(*routed, lw)
    return xf.reshape(batch, seq, D_MODEL)


def _prepare_layer(l, w_in, rpb, lam_re, lam_im, log_step, b_re, b_im, c_re, c_im, d_skip, w_glu, b_glu,
                   w_proj_a, w_proj_b, w_out, ln1_g, ln1_b, w_router, b_router, w_up, b_up, w_down,
                   b_down, ln2_g, ln2_b):
    wr_t = w_router[l].astype(F32).T
    wr_hi = wr_t.astype(BF16)
    wr_lo = (wr_t - wr_hi.astype(F32)).astype(BF16)
    row = lambda a: a.astype(F32).reshape(1, -1)
    return {
        "w_in": w_in[l].astype(BF16),
        "att_bias": _attention_bias(rpb[l]),
        "s5": _s5_weights(lam_re[l], lam_im[l], log_step[l], b_re[l], b_im[l], c_re[l], c_im[l]),
        "d_skip": row(d_skip[l]),
        "w_glu": w_glu[l].astype(BF16), "b_glu": row(b_glu[l]),
        "w_pa": w_proj_a[l].astype(BF16), "w_pb": w_proj_b[l].astype(BF16), "w_out": w_out[l].astype(BF16),
        "ln1_g": row(ln1_g[l]), "ln1_b": row(ln1_b[l]),
        "wr_hi": wr_hi, "wr_lo": wr_lo, "br": b_router[l].astype(F32).reshape(N_EXPERTS, 1),
        "layer": l,
        "w_up": w_up, "b_up": b_up[l].astype(F32).reshape(N_EXPERTS, 1, 2 * D_EXPERT),
        "w_down": w_down, "b_down": b_down[l].astype(F32).reshape(N_EXPERTS, 1, D_MODEL),
        "ln2_g": row(ln2_g[l]), "ln2_b": row(ln2_b[l]),
    }


def kernel(x_prompt, x_sample, w_in, rpb, lam_re, lam_im, log_step, b_re, b_im, c_re, c_im, d_skip, w_glu, b_glu, w_proj_a, w_proj_b, w_out, ln1_g, ln1_b, w_router, b_router, w_up, b_up, w_down, b_down, ln2_g, ln2_b):
    weights = (w_in, rpb, lam_re, lam_im, log_step, b_re, b_im, c_re, c_im, d_skip, w_glu, b_glu,
               w_proj_a, w_proj_b, w_out, ln1_g, ln1_b, w_router, b_router, w_up, b_up, w_down,
               b_down, ln2_g, ln2_b)
    layers = [_prepare_layer(l, *weights) for l in range(DEPTH)]
    return (_trunk(x_prompt, layers), _trunk(x_sample, layers))
```
